```python
import jax, jax.numpy as jnp
from jax import lax
import numpy as np

D_MODEL = 2048
BATCH = 2
SEQ = 4096
DEPTH = 2
DEC_BATCH = 128
DEC_SEQ = 8
PAST_LEN = 8192
PAGE_SIZE = 128

F32 = jnp.float32
D_CONV = D_MODEL
CONV_W = 3
R_HEADS = 8
R_DK = 128
R_DV = D_MODEL // R_HEADS
R_CHUNK = 128
R_THETA = 10000.0
A_HEADS = 32
A_KV = 4
A_HD = D_MODEL // A_HEADS
A_GROUP = A_HEADS // A_KV
WINDOW = 128
ROPE_DIMS = A_HD // 4
ROPE_THETA = 500000.0
N_MEM = 256
M_HEADS = 4
M_HD = D_MODEL // M_HEADS
N_EXP = 32
TOP_K = 4
D_FF = D_MODEL
SWIGLU_LIMIT = 7.0
SWIGLU_ALPHA = 1.702
MOE_BLOCK = 128
N_BRANCH = 4
LN_EPS = 1e-5
GN_EPS = 1e-6
DN_ALPHA = (2 * DEPTH) ** 0.25
DN_BETA = (8 * DEPTH) ** -0.25
IN_SIZES = (D_CONV, D_CONV, D_CONV, R_HEADS * R_DK, R_HEADS * R_DK, R_HEADS * R_DV, R_HEADS * R_DV, A_HEADS * A_HD, A_KV * A_HD, A_KV * A_HD, M_HEADS * M_HD, N_BRANCH * D_MODEL)
D_IN = sum(IN_SIZES)

kernel_name = 'hybrid_conv_retention_swa_mem_moe_step'


def _split_cols(z):
    parts, off = [], 0
    for n in IN_SIZES:
        parts.append(z[..., off:off + n])
        off += n
    return parts


def _layernorm(x, g, b):
    xf = x.astype(F32)
    mu = jnp.mean(xf, axis=-1, keepdims=True)
    var = jnp.mean(jnp.square(xf - mu), axis=-1, keepdims=True)
    return ((xf - mu) * lax.rsqrt(var + LN_EPS) * g + b).astype(x.dtype)


def _group_norm(o, g):
    B, T, H, dv = o.shape
    of = o.astype(F32)
    mu = jnp.mean(of, axis=-1, keepdims=True)
    var = jnp.mean(jnp.square(of - mu), axis=-1, keepdims=True)
    y = ((of - mu) * lax.rsqrt(var + GN_EPS)).reshape(B, T, H * dv) * g
    return y.astype(o.dtype)


def _rope(x, pos, n_rot, theta):
    half = n_rot // 2
    inv = theta ** (-jnp.arange(half, dtype=F32) / half)
    ang = pos.astype(F32)[:, None] * inv[None, :]
    cos = jnp.cos(ang)[:, None, :]
    sin = jnp.sin(ang)[:, None, :]
    xr = x[..., :n_rot].astype(F32)
    x1, x2 = xr[..., :half], xr[..., half:]
    rot = jnp.concatenate([x1 * cos - x2 * sin, x2 * cos + x1 * sin], axis=-1).astype(x.dtype)
    return jnp.concatenate([rot, x[..., n_rot:]], axis=-1)


def _short_conv(u, buf, w):
    T = u.shape[1]
    full = jnp.concatenate([buf, u], axis=1)
    y = full[:, 0:T] * w[0]
    for j in range(1, CONV_W):
        y = y + full[:, j:j + T] * w[j]
    return y, full[:, -(CONV_W - 1):]


def _retention(q, k, v, s0):
    B, T, H, _ = q.shape
    dv = v.shape[-1]
    C = R_CHUNK if T % R_CHUNK == 0 else T
    NC = T // C
    lg = jnp.log(1.0 - 2.0 ** (-5.0 - jnp.arange(H, dtype=F32)))
    i = jnp.arange(C, dtype=F32)
    diff = i[:, None] - i[None, :]
    dmask = jnp.where(diff >= 0, jnp.exp(lg[:, None, None] * jnp.maximum(diff, 0.0)), 0.0).astype(q.dtype)
    qdec = jnp.exp(lg[:, None] * (i + 1.0))[None, :, :, None].astype(q.dtype)
    kdec = jnp.exp(lg[:, None] * (C - 1.0 - i))[None, :, :, None].astype(q.dtype)
    cdec = jnp.exp(lg * C)[None, :, None, None].astype(q.dtype)

    def chunks(a):
        return a.reshape(B, NC, C, H, a.shape[-1]).transpose(1, 0, 3, 2, 4)

    def step(s, qkv):
        qc, kc, vc = qkv
        att = jnp.einsum('bhid,bhjd->bhij', qc, kc) * dmask
        o = jnp.einsum('bhij,bhje->bhie', att, vc) + jnp.einsum('bhid,bhde->bhie', qc * qdec, s)
        s_new = cdec * s + jnp.einsum('bhjd,bhje->bhde', kc * kdec, vc)
        return s_new.astype(s0.dtype), o

    s_fin, o = lax.scan(step, s0, (chunks(q), chunks(k), chunks(v)))
    o = o.transpose(1, 0, 3, 2, 4).reshape(B, T, H, dv)
    return o, s_fin


def _sink_softmax(s, valid, sink):
    s = jnp.where(valid, s, -jnp.inf)
    sink = sink.astype(F32)
    m = jnp.maximum(jnp.max(s, axis=-1, keepdims=True), sink)
    p = jnp.exp(s - m)
    return p / (jnp.sum(p, axis=-1, keepdims=True) + jnp.exp(sink - m))


def _swa_prompt(q, k, v, sinks):
    B, T = q.shape[0], q.shape[1]
    NB = T // WINDOW
    qb = q.reshape(B, NB, WINDOW, A_KV, A_GROUP, A_HD)

    def band(a):
        ab = a.reshape(B, NB, WINDOW, A_KV, A_HD)
        prev = jnp.pad(ab, ((0, 0), (1, 0), (0, 0), (0, 0), (0, 0)))[:, :-1]
        return jnp.concatenate([prev, ab], axis=2)

    kk, vv = band(k), band(v)
    s = jnp.einsum('bnqhgd,bnkhd->bnhgqk', qb, kk).astype(F32) * (A_HD ** -0.5)
    qi = jnp.arange(WINDOW)[:, None]
    kj = jnp.arange(2 * WINDOW)[None, :] - WINDOW
    diff = qi - kj
    local = (diff >= 0) & (diff <= WINDOW)
    inside = (jnp.arange(NB)[:, None, None] * WINDOW + kj[None]) >= 0
    valid = (local[None] & inside)[None, :, None, None]
    p = _sink_softmax(s, valid, sinks.reshape(A_KV, A_GROUP, 1, 1)).astype(v.dtype)
    o = jnp.einsum('bnhgqk,bnkhd->bnqhgd', p, vv).reshape(B, T, A_HEADS * A_HD)
    return o, k[:, -WINDOW:], v[:, -WINDOW:]


def _swa_sample(q, k, v, win_k, win_v, sinks):
    B, T = q.shape[0], q.shape[1]
    kk = jnp.concatenate([win_k, k], axis=1)
    vv = jnp.concatenate([win_v, v], axis=1)
    qg = q.reshape(B, T, A_KV, A_GROUP, A_HD)
    s = jnp.einsum('bqhgd,bkhd->bhgqk', qg, kk).astype(F32) * (A_HD ** -0.5)
    diff = (jnp.arange(T)[:, None] + WINDOW) - jnp.arange(WINDOW + T)[None, :]
    valid = (diff >= 0) & (diff <= WINDOW)
    p = _sink_softmax(s, valid, sinks.reshape(A_KV, A_GROUP, 1, 1)).astype(v.dtype)
    o = jnp.einsum('bhgqk,bkhd->bqhgd', p, vv).reshape(B, T, A_HEADS * A_HD)
    return o, kk[:, -WINDOW:], vv[:, -WINDOW:]


def _mem_kv(mem, w):
    B = mem.shape[0]
    kv = mem @ w
    mk = kv[..., :M_HEADS * M_HD].reshape(B, N_MEM, M_HEADS, M_HD)
    mv = kv[..., M_HEADS * M_HD:].reshape(B, N_MEM, M_HEADS, M_HD)
    return mk, mv


def _mem_attend(q, mk, mv):
    s = jnp.einsum('bthd,bmhd->bhtm', q, mk).astype(F32) * (M_HD ** -0.5)
    p = jax.nn.softmax(s, axis=-1).astype(mv.dtype)
    return jnp.einsum('bhtm,bmhd->bthd', p, mv).reshape(q.shape[0], q.shape[1], M_HEADS * M_HD)


def _moe(x, router_w, router_b, w_gu, b_gu, w_dn, b_dn):
    B, T, D = x.shape
    xf = x.reshape(-1, D)
    N = xf.shape[0]
    logits = (xf @ router_w).astype(F32) + router_b.astype(F32)
    top_v, top_i = lax.top_k(logits, TOP_K)
    gates = jax.nn.softmax(top_v, axis=-1).astype(x.dtype)
    NA = N * TOP_K
    flat_e = top_i.reshape(-1).astype(jnp.int32)
    flat_tok = jnp.arange(NA, dtype=jnp.int32) // TOP_K
    flat_g = gates.reshape(-1)
    order = jnp.argsort(flat_e)
    se = flat_e[order]
    counts = jnp.bincount(flat_e, length=N_EXP).astype(jnp.int32)
    padded = (counts + MOE_BLOCK - 1) // MOE_BLOCK * MOE_BLOCK
    pend = jnp.cumsum(padded)
    pstart = pend - padded
    ustart = jnp.cumsum(counts) - counts
    dest = pstart[se] + jnp.arange(NA, dtype=jnp.int32) - ustart[se]
    n_blocks = -(-NA // MOE_BLOCK) + N_EXP
    P = n_blocks * MOE_BLOCK
    row_tok = jnp.full((P,), N, dtype=jnp.int32).at[dest].set(flat_tok[order])
    row_g = jnp.zeros((P,), dtype=x.dtype).at[dest].set(flat_g[order])
    blk_e = jnp.minimum(jnp.searchsorted(pend, jnp.arange(n_blocks, dtype=jnp.int32) * MOE_BLOCK, side='right'), N_EXP - 1)
    xpad = jnp.concatenate([xf, jnp.zeros((1, D), dtype=xf.dtype)], axis=0)

    def expert_block(args):
        tok, g, e = args
        hgu = xpad[tok] @ w_gu[e] + b_gu[e]
        gate = jnp.minimum(hgu[:, :D_FF], SWIGLU_LIMIT)
        up = jnp.clip(hgu[:, D_FF:], -SWIGLU_LIMIT, SWIGLU_LIMIT)
        h = (up + 1.0) * (gate * jax.nn.sigmoid(SWIGLU_ALPHA * gate))
        return (h @ w_dn[e] + b_dn[e]) * g[:, None]

    yb = lax.map(expert_block, (row_tok.reshape(n_blocks, MOE_BLOCK), row_g.reshape(n_blocks, MOE_BLOCK), blk_e))
    y = jax.ops.segment_sum(yb.reshape(P, D), row_tok, num_segments=N + 1)[:N]
    return y.reshape(B, T, D)


def _layer(x, pos, conv_buf, ret_s, win_k, win_v, mem_k, mem_v, p):
    (w_in, conv_w, gn_w, sinks, w_o, ln1_g, ln1_b, router_w, router_b, w_gu, b_gu, w_dn, b_dn, ln2_g, ln2_b) = p
    B, T, _ = x.shape
    h_c, b_c, c_c, rq, rk, rv, rg, aq, ak, av, mq, gl = _split_cols(x @ w_in)
    conv_y, conv_buf = _short_conv(c_c * h_c, conv_buf, conv_w)
    y_a = b_c * conv_y
    rq = _rope(rq.reshape(B, T, R_HEADS, R_DK), pos, R_DK, R_THETA)
    rk = _rope(rk.reshape(B, T, R_HEADS, R_DK), pos, R_DK, R_THETA) * (R_DK ** -0.5)
    o_r, ret_s = _retention(rq, rk, rv.reshape(B, T, R_HEADS, R_DV), ret_s)
    y_b = jax.nn.silu(rg) * _group_norm(o_r, gn_w)
    aq = _rope(aq.reshape(B, T, A_HEADS, A_HD), pos, ROPE_DIMS, ROPE_THETA)
    ak = _rope(ak.reshape(B, T, A_KV, A_HD), pos, ROPE_DIMS, ROPE_THETA)
    av = av.reshape(B, T, A_KV, A_HD)
    if win_k is None:
        y_c, win_k, win_v = _swa_prompt(aq, ak, av, sinks)
    else:
        y_c, win_k, win_v = _swa_sample(aq, ak, av, win_k, win_v, sinks)
    y_m = _mem_attend(mq.reshape(B, T, M_HEADS, M_HD), mem_k, mem_v)
    g = jax.nn.sigmoid(gl.reshape(B, T, N_BRANCH, D_MODEL))
    merged = g[:, :, 0] * y_a + g[:, :, 1] * y_b + g[:, :, 2] * y_c + g[:, :, 3] * y_m
    x = _layernorm(DN_ALPHA * x + merged @ w_o, ln1_g, ln1_b)
    x = _layernorm(DN_ALPHA * x + _moe(x, router_w, router_b, w_gu, b_gu, w_dn, b_dn), ln2_g, ln2_b)
    return x, conv_buf, ret_s, win_k, win_v


def setup_inputs(seed: int = 0) -> dict:
    key = jax.random.key(seed)
    ks = jax.random.split(key, 26)

    def nrm(k, shape, s):
        return jax.random.normal(k, shape, F32) * s

    return {
        'x_prompt': nrm(ks[0], (BATCH, SEQ, D_MODEL), 1.0),
        'x_sample': nrm(ks[1], (DEC_BATCH, DEC_SEQ, D_MODEL), 1.0),
        'cache_conv': nrm(ks[2], (DEPTH, DEC_BATCH, CONV_W - 1, D_CONV), 1.0),
        'state_ret': nrm(ks[3], (DEPTH, DEC_BATCH, R_HEADS, R_DK, R_DV), 1.0),
        'cache_swa_k': nrm(ks[4], (DEPTH, DEC_BATCH, WINDOW, A_KV, A_HD), 1.0),
        'cache_swa_v': nrm(ks[5], (DEPTH, DEC_BATCH, WINDOW, A_KV, A_HD), 1.0),
        'cache_mem_k': nrm(ks[6], (DEPTH, DEC_BATCH, N_MEM, M_HEADS, M_HD), 1.0),
        'cache_mem_v': nrm(ks[7], (DEPTH, DEC_BATCH, N_MEM, M_HEADS, M_HD), 1.0),
        'mem_prompt': nrm(ks[8], (BATCH, N_MEM, D_MODEL), 1.0),
        'w_in': nrm(ks[9], (DEPTH, D_MODEL, D_IN), D_MODEL ** -0.5),
        'conv_w': nrm(ks[10], (DEPTH, CONV_W, D_CONV), CONV_W ** -0.5),
        'ret_gn_w': 1.0 + nrm(ks[11], (DEPTH, R_HEADS * R_DV), 0.02),
        'attn_sinks': nrm(ks[12], (DEPTH, A_HEADS), 0.5),
        'w_mem_kv': nrm(ks[13], (DEPTH, D_MODEL, 2 * M_HEADS * M_HD), D_MODEL ** -0.5),
        'w_o': nrm(ks[14], (DEPTH, D_MODEL, D_MODEL), DN_BETA * D_MODEL ** -0.5),
        'ln1_g': 1.0 + nrm(ks[15], (DEPTH, D_MODEL), 0.02),
        'ln1_b': nrm(ks[16], (DEPTH, D_MODEL), 0.02),
        'router_w': nrm(ks[17], (DEPTH, D_MODEL, N_EXP), D_MODEL ** -0.5),
        'router_b': nrm(ks[18], (DEPTH, N_EXP), 0.01),
        'w_gate_up': nrm(ks[19], (DEPTH, N_EXP, D_MODEL, 2 * D_FF), D_MODEL ** -0.5),
        'b_gate_up': nrm(ks[20], (DEPTH, N_EXP, 2 * D_FF), 0.01),
        'w_down': nrm(ks[21], (DEPTH, N_EXP, D_FF, D_MODEL), DN_BETA * D_FF ** -0.5),
        'b_down': nrm(ks[22], (DEPTH, N_EXP, D_MODEL), 0.01),
        'ln2_g': 1.0 + nrm(ks[23], (DEPTH, D_MODEL), 0.02),
        'ln2_b': nrm(ks[24], (DEPTH, D_MODEL), 0.02),
    }


def reference(x_prompt, x_sample, cache_conv, state_ret, cache_swa_k, cache_swa_v, cache_mem_k, cache_mem_v, mem_prompt, w_in, conv_w, ret_gn_w, attn_sinks, w_mem_kv, w_o, ln1_g, ln1_b, router_w, router_b, w_gate_up, b_gate_up, w_down, b_down, ln2_g, ln2_b):
    Bp, Tp = x_prompt.shape[0], x_prompt.shape[1]
    Ts = x_sample.shape[1]
    pos_p = jnp.arange(Tp, dtype=jnp.int32)
    pos_s = PAST_LEN + jnp.arange(Ts, dtype=jnp.int32)
    hp, hs = x_prompt, x_sample
    p_conv, p_ret, p_k, p_v, p_mk, p_mv = [], [], [], [], [], []
    s_conv, s_ret, s_k, s_v = [], [], [], []
    for l in range(DEPTH):
        params = (w_in[l], conv_w[l], ret_gn_w[l], attn_sinks[l], w_o[l], ln1_g[l], ln1_b[l], router_w[l], router_b[l], w_gate_up[l], b_gate_up[l], w_down[l], b_down[l], ln2_g[l], ln2_b[l])
        mk_p, mv_p = _mem_kv(mem_prompt, w_mem_kv[l])
        conv0 = jnp.zeros((Bp, CONV_W - 1, D_CONV), dtype=x_prompt.dtype)
        s0 = jnp.zeros((Bp, R_HEADS, R_DK, R_DV), dtype=x_prompt.dtype)
        hp, cb, rs, wk, wv = _layer(hp, pos_p, conv0, s0, None, None, mk_p, mv_p, params)
        p_conv.append(cb); p_ret.append(rs); p_k.append(wk); p_v.append(wv); p_mk.append(mk_p); p_mv.append(mv_p)
        hs, cb, rs, wk, wv = _layer(hs, pos_s, cache_conv[l], state_ret[l], cache_swa_k[l], cache_swa_v[l], cache_mem_k[l], cache_mem_v[l], params)
        s_conv.append(cb); s_ret.append(rs); s_k.append(wk); s_v.append(wv)
    return (hp, hs, jnp.stack(p_conv), jnp.stack(p_ret), jnp.stack(p_k), jnp.stack(p_v), jnp.stack(p_mk), jnp.stack(p_mv), jnp.stack(s_conv), jnp.stack(s_ret), jnp.stack(s_k), jnp.stack(s_v))
```

```python
import functools

import jax
import jax.numpy as jnp
from jax import lax
from jax.experimental import pallas as pl
from jax.experimental.pallas import tpu as pltpu

F32 = jnp.float32
BF16 = jnp.bfloat16

D_MODEL = 2048
BATCH = 2
SEQ = 4096
DEPTH = 2
DEC_BATCH = 128
DEC_SEQ = 8
PAST_LEN = 8192
N_PROMPT = BATCH * SEQ
N_SAMPLE = DEC_BATCH * DEC_SEQ
N_TOK = N_PROMPT + N_SAMPLE

CONV_W = 3
R_HEADS = 8
R_DK = 128
R_DV = D_MODEL // R_HEADS
R_CHUNK = 128
R_THETA = 10000.0
A_HEADS = 32
A_KV = 4
A_HD = D_MODEL // A_HEADS
A_GROUP = A_HEADS // A_KV
WINDOW = 128
ROPE_DIMS = A_HD // 4
ROPE_THETA = 500000.0
N_MEM = 256
M_HEADS = 4
M_HD = D_MODEL // M_HEADS
N_EXP = 32
TOP_K = 4
D_FF = D_MODEL
SWIGLU_LIMIT = 7.0
SWIGLU_ALPHA = 1.702
LN_EPS = 1e-5
GN_EPS = 1e-6
DN_ALPHA = (2 * DEPTH) ** 0.25

REF_IN_SIZES = (D_MODEL, D_MODEL, D_MODEL, R_HEADS * R_DK, R_HEADS * R_DK, D_MODEL, D_MODEL,
                D_MODEL, A_KV * A_HD, A_KV * A_HD, D_MODEL, 4 * D_MODEL)
D_IN = sum(REF_IN_SIZES)
MY_ORDER = (11, 0, 1, 2, 5, 6, 7, 10, 3, 4, 8, 9)
GL, HC, BC, CC, RV, RG, AQ, MQ, RQ, RK, AK, AV = (
    0, 8192, 10240, 12288, 14336, 16384, 18432, 20480, 22528, 23552, 24576, 24832)

LANES = 128
SUBLANES = 8
VMEM_LIMIT = 56 * 1024 * 1024

MOE_TB = 256
MOE_NB = N_TOK * TOP_K // MOE_TB + N_EXP
MOE_P = MOE_NB * MOE_TB
MOE_TF = 1024
CMB_T = 64


def _cparams(sem, vmem=None):
    return pltpu.CompilerParams(dimension_semantics=sem, vmem_limit_bytes=vmem)


def _mm_kernel(x_ref, w_ref, o_ref):
    o_ref[...] = jnp.dot(x_ref[...], w_ref[...], preferred_element_type=F32).astype(o_ref.dtype)


def _matmul(x, w, tm, tn, out_dtype=F32):
    m, k = x.shape
    n = w.shape[1]
    return pl.pallas_call(
        _mm_kernel,
        grid=(m // tm, n // tn),
        in_specs=[pl.BlockSpec((tm, k), lambda i, j: (i, 0)),
                  pl.BlockSpec((k, tn), lambda i, j: (0, j))],
        out_specs=pl.BlockSpec((tm, tn), lambda i, j: (i, j)),
        out_shape=jax.ShapeDtypeStruct((m, n), out_dtype),
        compiler_params=_cparams(("parallel", "parallel"), VMEM_LIMIT),
        name="matmul",
    )(x, w)


def _conv_kernel(seq_len, full_u, hc_ref, cc_ref, bc_ref, g_ref, hh_ref, ch_ref, ea_ref, eb_ref,
                 w_ref, ya_ref, tail_ref):
    i = pl.program_id(0)
    tm = hc_ref.shape[0]
    u = cc_ref[...] * hc_ref[...]
    ue = jnp.concatenate([ch_ref[...] * hh_ref[...], u], axis=0)
    p1 = pltpu.roll(ue, 1, 0)[SUBLANES:]
    p2 = pltpu.roll(ue, 2, 0)[SUBLANES:]
    t = (i * tm + lax.broadcasted_iota(jnp.int32, (tm, 1), 0)) % seq_len
    ea = ea_ref[...]
    eb = eb_ref[...]
    p1 = jnp.where(t == 0, eb, p1)
    p2 = jnp.where(t == 0, ea, jnp.where(t == 1, eb, p2))
    w = w_ref[...]
    y = p2 * w[0:1] + p1 * w[1:2] + u * w[2:3]
    ya_ref[...] = jax.nn.sigmoid(g_ref[...]) * (bc_ref[...] * y)
    tail_ref[...] = u if full_u else u[tm - SUBLANES:]


def _conv_branch(z, conv_w, ea, eb, row0, rows, seq_len, sample):
    tm, tc = 256, 512
    r0 = row0 // tm
    nj = D_MODEL // tc

    def zspec(col):
        return pl.BlockSpec((tm, tc), lambda i, j: (r0 + i, col // tc + j))

    def halo(col):
        return pl.BlockSpec(
            (SUBLANES, tc),
            lambda i, j: (jnp.maximum((r0 + i) * (tm // SUBLANES) - 1, 0), col // tc + j))

    if sample:
        espec = pl.BlockSpec((tm, tc), lambda i, j: (i, j))
        tail_spec = pl.BlockSpec((tm, tc), lambda i, j: (i, j))
        tail_shape = jax.ShapeDtypeStruct((rows, D_MODEL), F32)
    else:
        espec = pl.BlockSpec((tm, tc), lambda i, j: (0, j))
        tail_spec = pl.BlockSpec((SUBLANES, tc), lambda i, j: (i, j))
        tail_shape = jax.ShapeDtypeStruct((rows // tm * SUBLANES, D_MODEL), F32)
    return pl.pallas_call(
        functools.partial(_conv_kernel, seq_len, sample),
        grid=(rows // tm, nj),
        in_specs=[zspec(HC), zspec(CC), zspec(BC), zspec(GL), halo(HC), halo(CC), espec, espec,
                  pl.BlockSpec((CONV_W, tc), lambda i, j: (0, j))],
        out_specs=[pl.BlockSpec((tm, tc), lambda i, j: (i, j)), tail_spec],
        out_shape=[jax.ShapeDtypeStruct((rows, D_MODEL), F32), tail_shape],
        compiler_params=_cparams(("parallel", "parallel"), VMEM_LIMIT),
        name="conv_sample" if sample else "conv_prompt",
    )(z, z, z, z, z, z, ea, eb, conv_w)


def _rope_full(x, cos, sin):
    return x * cos + pltpu.roll(x, R_DK // 2, 1) * sin


def _bf16_exact(x):
    return x.astype(BF16).astype(F32)


def _group_norm_gate(o, gn, rg, g):
    mu = jnp.mean(o, axis=-1, keepdims=True)
    d = o - mu
    var = jnp.mean(d * d, axis=-1, keepdims=True)
    yn = d * lax.rsqrt(var + GN_EPS) * gn
    return jax.nn.sigmoid(g) * ((rg * jax.nn.sigmoid(rg)) * yn)


def _ret_prompt_kernel(q_ref, k_ref, v_ref, rg_ref, g_ref, cos_ref, sin_ref, dmask_ref, qdec_ref,
                       kdec_ref, cdec_ref, gn_ref, y_ref, sfin_ref, s_scr):
    c = pl.program_id(2)

    @pl.when(c == 0)
    def _():
        s_scr[...] = jnp.zeros_like(s_scr)

    cos = cos_ref[...]
    sin = sin_ref[...]
    q = _rope_full(q_ref[...], cos, sin)
    k = _rope_full(k_ref[...], cos, sin) * (R_DK ** -0.5)
    vb = v_ref[...].astype(BF16)
    s = s_scr[...]
    att = lax.dot_general(q.astype(BF16), k.astype(BF16), (((1,), (1,)), ((), ())),
                          preferred_element_type=F32) * dmask_ref[0]
    o = (jnp.dot(att.astype(BF16), vb, preferred_element_type=F32)
         + jnp.dot((q * qdec_ref[0]).astype(BF16), s.astype(BF16), preferred_element_type=F32))
    kd = (k * kdec_ref[0]).astype(BF16)
    s_new = cdec_ref[0] * s + lax.dot_general(kd, vb, (((0,), (0,)), ((), ())),
                                              preferred_element_type=F32)
    s_scr[...] = s_new
    sfin_ref[0, 0] = s_new
    y_ref[...] = _group_norm_gate(o, gn_ref[...], rg_ref[...], g_ref[...])


def _ret_prompt(z, tabs, gn_w):
    cos, sin, dmask, qdec, kdec, cdec = tabs
    nc = SEQ // R_CHUNK

    def zspec(col, width):
        return pl.BlockSpec((R_CHUNK, width), lambda b, h, c: (b * nc + c, col // width + h))

    def hspec(shape):
        return pl.BlockSpec((1,) + shape, lambda b, h, c: (h, 0, 0))

    return pl.pallas_call(
        _ret_prompt_kernel,
        grid=(BATCH, R_HEADS, nc),
        in_specs=[zspec(RQ, R_DK), zspec(RK, R_DK), zspec(RV, R_DV), zspec(RG, R_DV),
                  zspec(GL + D_MODEL, R_DV),
                  pl.BlockSpec((R_CHUNK, R_DK), lambda b, h, c: (c, 0)),
                  pl.BlockSpec((R_CHUNK, R_DK), lambda b, h, c: (c, 0)),
                  hspec((R_CHUNK, R_CHUNK)), hspec((R_CHUNK, R_DK)), hspec((R_CHUNK, R_DK)),
                  hspec((1, R_DV)),
                  pl.BlockSpec((1, R_DV), lambda b, h, c: (0, h))],
        out_specs=[pl.BlockSpec((R_CHUNK, R_DV), lambda b, h, c: (b * nc + c, h)),
                   pl.BlockSpec((1, 1, R_DK, R_DV), lambda b, h, c: (b, h, 0, 0))],
        out_shape=[jax.ShapeDtypeStruct((N_PROMPT, D_MODEL), F32),
                   jax.ShapeDtypeStruct((BATCH, R_HEADS, R_DK, R_DV), F32)],
        scratch_shapes=[pltpu.VMEM((R_DK, R_DV), F32)],
        compiler_params=_cparams(("parallel", "parallel", "arbitrary")),
        name="retention_prompt",
    )(z, z, z, z, z, cos, sin, dmask, qdec, kdec, cdec, gn_w)


def _ret_sample_kernel(q_ref, k_ref, v_ref, rg_ref, g_ref, cos_ref, sin_ref, dmask_ref, qdec_ref,
                       kdec_ref, cdec_ref, gn_ref, s_ref, y_ref, snew_ref):
    cos = cos_ref[...]
    sin = sin_ref[...]
    for h in range(R_HEADS):
        qs = slice(h * R_DK, (h + 1) * R_DK)
        vs = slice(h * R_DV, (h + 1) * R_DV)
        q = _rope_full(q_ref[:, qs], cos, sin)
        k = _rope_full(k_ref[:, qs], cos, sin) * (R_DK ** -0.5)
        v = _bf16_exact(v_ref[:, vs])
        s = s_ref[0, h]
        att = lax.dot_general(_bf16_exact(q), _bf16_exact(k), (((1,), (1,)), ((), ())),
                              preferred_element_type=F32) * dmask_ref[h]
        o = (jnp.dot(_bf16_exact(att), v, preferred_element_type=F32)
             + jnp.dot((q * qdec_ref[h]).astype(BF16), s.astype(BF16), preferred_element_type=F32))
        kd = _bf16_exact(k * kdec_ref[h])
        snew_ref[0, h] = cdec_ref[h] * s + lax.dot_general(
            kd, v, (((0,), (0,)), ((), ())), preferred_element_type=F32)
        y_ref[:, vs] = _group_norm_gate(o, gn_ref[:, vs], rg_ref[:, vs], g_ref[:, vs])


def _ret_sample(z, tabs, gn_w, state):
    cos, sin, dmask, qdec, kdec, cdec = tabs
    r0 = N_PROMPT // DEC_SEQ

    def zspec(col, width):
        return pl.BlockSpec((DEC_SEQ, width), lambda b: (r0 + b, col // width))

    def full(a):
        return pl.BlockSpec(a.shape, lambda b: (0,) * a.ndim)

    sspec = pl.BlockSpec((1, R_HEADS, R_DK, R_DV), lambda b: (b, 0, 0, 0))
    return pl.pallas_call(
        _ret_sample_kernel,
        grid=(DEC_BATCH,),
        in_specs=[zspec(RQ, R_HEADS * R_DK), zspec(RK, R_HEADS * R_DK), zspec(RV, D_MODEL),
                  zspec(RG, D_MODEL), zspec(GL + D_MODEL, D_MODEL),
                  full(cos), full(sin), full(dmask), full(qdec), full(kdec), full(cdec), full(gn_w),
                  sspec],
        out_specs=[pl.BlockSpec((DEC_SEQ, D_MODEL), lambda b: (b, 0)), sspec],
        out_shape=[jax.ShapeDtypeStruct((N_SAMPLE, D_MODEL), F32),
                   jax.ShapeDtypeStruct(state.shape, F32)],
        compiler_params=_cparams(("parallel",)),
        name="retention_sample",
    )(z, z, z, z, z, cos, sin, dmask, qdec, kdec, cdec, gn_w, state)


def _ret_tables(pos, chunk):
    half = R_DK // 2
    inv = R_THETA ** (-jnp.arange(half, dtype=F32) / half)
    ang = pos.astype(F32)[:, None] * inv[None, :]
    cos = jnp.concatenate([jnp.cos(ang), jnp.cos(ang)], axis=-1)
    sin = jnp.concatenate([-jnp.sin(ang), jnp.sin(ang)], axis=-1)
    lg = jnp.log(1.0 - 2.0 ** (-5.0 - jnp.arange(R_HEADS, dtype=F32)))
    i = jnp.arange(chunk, dtype=F32)
    diff = i[:, None] - i[None, :]
    dmask = jnp.where(diff >= 0, jnp.exp(lg[:, None, None] * jnp.maximum(diff, 0.0)), 0.0)
    qdec = jnp.broadcast_to(jnp.exp(lg[:, None] * (i + 1.0))[:, :, None], (R_HEADS, chunk, R_DK))
    kdec = jnp.broadcast_to(jnp.exp(lg[:, None] * (chunk - 1.0 - i))[:, :, None],
                            (R_HEADS, chunk, R_DK))
    cdec = jnp.broadcast_to(jnp.exp(lg * chunk)[:, None, None], (R_HEADS, 1, R_DV))
    return cos, sin, dmask.astype(F32), qdec.astype(F32), kdec.astype(F32), cdec.astype(F32)


def _swa_tables(pos):
    half = ROPE_DIMS // 2
    inv = ROPE_THETA ** (-jnp.arange(half, dtype=F32) / half)
    ang = pos.astype(F32)[:, None] * inv[None, :]
    cos, sin = jnp.cos(ang), jnp.sin(ang)
    t = pos.shape[0]
    ones = jnp.ones((t, A_HD - ROPE_DIMS), F32)
    zeros = jnp.zeros((t, A_HD - ROPE_DIMS), F32)
    zh = jnp.zeros((t, half), F32)
    c = jnp.concatenate([cos, cos, ones], axis=-1)
    s_lo = jnp.concatenate([-sin, zh, zeros], axis=-1)
    s_hi = jnp.concatenate([zh, sin, zeros], axis=-1)
    rep = LANES // A_HD
    return jnp.tile(c, (1, rep)), jnp.tile(s_lo, (1, rep)), jnp.tile(s_hi, (1, rep))


def _rope_partial(x, c, s_lo, s_hi):
    half = ROPE_DIMS // 2
    return x * c + pltpu.roll(x, LANES - half, 1) * s_lo + pltpu.roll(x, half, 1) * s_hi


def _dup_head(chunk, odd):
    lane = lax.broadcasted_iota(jnp.int32, chunk.shape, 1)
    sw = pltpu.roll(chunk, A_HD, 1)
    if odd:
        return jnp.where(lane < A_HD, sw, chunk)
    return jnp.where(lane < A_HD, chunk, sw)


def _swa_core(sink_ref, q_ref, g_ref, y_ref, kp, kc, vp, vc, tabs_q, has_prev, tq, chunks_per_dot):
    cq, slq, shq = tabs_q
    lane_q = lax.broadcasted_iota(jnp.int32, (tq, LANES), 1)
    n_chunks = A_GROUP * A_HD // LANES
    for h in range(A_KV):
        kchunk = slice((h // 2) * LANES, (h // 2 + 1) * LANES)
        kkp = _dup_head(kp[:, kchunk], h % 2).astype(BF16)
        kkc = _dup_head(kc[:, kchunk], h % 2).astype(BF16)
        vvp = _dup_head(vp[:, kchunk], h % 2).astype(BF16)
        vvc = _dup_head(vc[:, kchunk], h % 2).astype(BF16)
        for c0 in range(0, n_chunks, chunks_per_dot):
            pieces, sinks = [], []
            for ci in range(c0, c0 + chunks_per_dot):
                col = h * A_GROUP * A_HD + ci * LANES
                qc = _rope_partial(q_ref[:, col:col + LANES], cq, slq, shq) * (A_HD ** -0.5)
                for half in range(2):
                    keep = (lane_q < A_HD) if half == 0 else (lane_q >= A_HD)
                    pieces.append(jnp.where(keep, qc, 0.0).astype(BF16))
                    head = h * A_GROUP + 2 * ci + half
                    sinks.append(jnp.full((tq, 1), sink_ref[head], F32))
            qs = jnp.concatenate(pieces, axis=0)
            sink = jnp.concatenate(sinks, axis=0)
            r = qs.shape[0]
            nt = (((1,), (1,)), ((), ()))
            sp = lax.dot_general(qs, kkp, nt, preferred_element_type=F32)
            sc = lax.dot_general(qs, kkc, nt, preferred_element_type=F32)
            t = lax.broadcasted_iota(jnp.int32, (r, WINDOW), 0) % tq
            j = lax.broadcasted_iota(jnp.int32, (r, WINDOW), 1)
            sp = jnp.where((j >= t) & has_prev, sp, -jnp.inf)
            sc = jnp.where(j <= t, sc, -jnp.inf)
            m = jnp.maximum(jnp.maximum(jnp.max(sp, axis=-1, keepdims=True),
                                        jnp.max(sc, axis=-1, keepdims=True)), sink)
            pp = jnp.exp(sp - m)
            pc = jnp.exp(sc - m)
            den = (jnp.sum(pp, axis=-1, keepdims=True) + jnp.sum(pc, axis=-1, keepdims=True)
                   + jnp.exp(sink - m))
            inv = 1.0 / den
            o = (jnp.dot((pp * inv).astype(BF16), vvp, preferred_element_type=F32)
                 + jnp.dot((pc * inv).astype(BF16), vvc, preferred_element_type=F32))
            for n, ci in enumerate(range(c0, c0 + chunks_per_dot)):
                col = h * A_GROUP * A_HD + ci * LANES
                oa = o[(2 * n) * tq:(2 * n + 1) * tq]
                ob = o[(2 * n + 1) * tq:(2 * n + 2) * tq]
                oc = jnp.where(lane_q < A_HD, oa, ob)
                y_ref[:, col:col + LANES] = jax.nn.sigmoid(g_ref[:, col:col + LANES]) * oc


def _rope_kv(k_ref, ck, slk, shk):
    parts = [_rope_partial(k_ref[:, c * LANES:(c + 1) * LANES], ck, slk, shk)
             for c in range(A_KV * A_HD // LANES)]
    return jnp.concatenate(parts, axis=-1)


def _swa_prompt_kernel(sink_ref, q_ref, k_ref, v_ref, g_ref, c_ref, sl_ref, sh_ref,
                       y_ref, kr_ref, kp_scr, vp_scr):
    n = pl.program_id(1)

    @pl.when(n == 0)
    def _():
        kp_scr[...] = jnp.zeros_like(kp_scr)
        vp_scr[...] = jnp.zeros_like(vp_scr)

    tabs = (c_ref[...], sl_ref[...], sh_ref[...])
    kc = _rope_kv(k_ref, *tabs)
    vc = v_ref[...]
    kr_ref[...] = kc
    _swa_core(sink_ref, q_ref, g_ref, y_ref, kp_scr[...], kc, vp_scr[...], vc, tabs, n > 0,
              WINDOW, 1)
    kp_scr[...] = kc
    vp_scr[...] = vc


def _swa_prompt(z, sinks, tabs):
    nb = SEQ // WINDOW
    kvw = A_KV * A_HD

    def zspec(col, width):
        return pl.BlockSpec((WINDOW, width), lambda b, n: (b * nb + n, col // width))

    tspec = pl.BlockSpec((WINDOW, LANES), lambda b, n: (n, 0))
    return pl.pallas_call(
        _swa_prompt_kernel,
        grid=(BATCH, nb),
        in_specs=[pl.BlockSpec(memory_space=pltpu.SMEM),
                  zspec(AQ, D_MODEL), zspec(AK, kvw), zspec(AV, kvw), zspec(GL + 2 * D_MODEL, D_MODEL),
                  tspec, tspec, tspec],
        out_specs=[pl.BlockSpec((WINDOW, D_MODEL), lambda b, n: (b * nb + n, 0)),
                   pl.BlockSpec((WINDOW, kvw), lambda b, n: (b * nb + n, 0))],
        out_shape=[jax.ShapeDtypeStruct((N_PROMPT, D_MODEL), F32),
                   jax.ShapeDtypeStruct((N_PROMPT, kvw), F32)],
        scratch_shapes=[pltpu.VMEM((WINDOW, kvw), F32), pltpu.VMEM((WINDOW, kvw), F32)],
        compiler_params=_cparams(("parallel", "arbitrary"), VMEM_LIMIT),
        name="swa_prompt",
    )(sinks, z, z, z, z, *tabs)


def _swa_sample_kernel(sink_ref, q_ref, k_ref, v_ref, g_ref, c_ref, sl_ref, sh_ref, ck_ref, cv_ref,
                       y_ref, nk_ref, nv_ref):
    tabs = (c_ref[...], sl_ref[...], sh_ref[...])
    kn = _rope_kv(k_ref, *tabs)
    vn = v_ref[...]
    pad = jnp.zeros((WINDOW - DEC_SEQ, A_KV * A_HD), F32)
    kp = ck_ref[0]
    vp = cv_ref[0]
    _swa_core(sink_ref, q_ref, g_ref, y_ref, kp, jnp.concatenate([kn, pad], axis=0),
              vp, jnp.concatenate([vn, pad], axis=0), tabs, True, DEC_SEQ,
              A_GROUP * A_HD // LANES)
    nk_ref[0] = jnp.concatenate([kp[DEC_SEQ:], kn], axis=0)
    nv_ref[0] = jnp.concatenate([vp[DEC_SEQ:], vn], axis=0)


def _swa_sample(z, sinks, tabs, cache_k, cache_v):
    kvw = A_KV * A_HD
    r0 = N_PROMPT // DEC_SEQ

    def zspec(col, width):
        return pl.BlockSpec((DEC_SEQ, width), lambda b: (r0 + b, col // width))

    tspec = pl.BlockSpec((DEC_SEQ, LANES), lambda b: (0, 0))
    cspec = pl.BlockSpec((1, WINDOW, kvw), lambda b: (b, 0, 0))
    return pl.pallas_call(
        _swa_sample_kernel,
        grid=(DEC_BATCH,),
        in_specs=[pl.BlockSpec(memory_space=pltpu.SMEM),
                  zspec(AQ, D_MODEL), zspec(AK, kvw), zspec(AV, kvw), zspec(GL + 2 * D_MODEL, D_MODEL),
                  tspec, tspec, tspec, cspec, cspec],
        out_specs=[pl.BlockSpec((DEC_SEQ, D_MODEL), lambda b: (b, 0)), cspec, cspec],
        out_shape=[jax.ShapeDtypeStruct((N_SAMPLE, D_MODEL), F32),
                   jax.ShapeDtypeStruct((DEC_BATCH, WINDOW, kvw), F32),
                   jax.ShapeDtypeStruct((DEC_BATCH, WINDOW, kvw), F32)],
        compiler_params=_cparams(("parallel",)),
        name="swa_sample",
    )(sinks, z, z, z, z, *tabs, cache_k, cache_v)


def _mem_kernel(q_ref, k_ref, v_ref, g_ref, y_ref):
    for h in range(M_HEADS):
        hs = slice(h * M_HD, (h + 1) * M_HD)
        q = q_ref[:, hs].astype(BF16)
        k = k_ref[0, :, hs].astype(BF16)
        v = v_ref[0, :, hs].astype(BF16)
        s = lax.dot_general(q, k, (((1,), (1,)), ((), ())), preferred_element_type=F32) * (M_HD ** -0.5)
        m = jnp.max(s, axis=-1, keepdims=True)
        p = jnp.exp(s - m)
        p = p / jnp.sum(p, axis=-1, keepdims=True)
        o = jnp.dot(p.astype(BF16), v, preferred_element_type=F32)
        y_ref[:, hs] = jax.nn.sigmoid(g_ref[:, hs]) * o


def _mem_attend(z, mk, mv, row0, rows_per_b, tq, name):
    nb = mk.shape[0]
    nq = rows_per_b // tq
    r0 = row0 // tq

    def zspec(col):
        return pl.BlockSpec((tq, D_MODEL), lambda b, i: (r0 + b * nq + i, col // D_MODEL))

    kvspec = pl.BlockSpec((1, N_MEM, D_MODEL), lambda b, i: (b, 0, 0))
    return pl.pallas_call(
        _mem_kernel,
        grid=(nb, nq),
        in_specs=[zspec(MQ), kvspec, kvspec, zspec(GL + 3 * D_MODEL)],
        out_specs=pl.BlockSpec((tq, D_MODEL), lambda b, i: (b * nq + i, 0)),
        out_shape=jax.ShapeDtypeStruct((nb * rows_per_b, D_MODEL), F32),
        compiler_params=_cparams(("parallel", "parallel"), VMEM_LIMIT),
        name=name,
    )(z, mk, mv, z)


def _layernorm(r, g, b):
    mu = jnp.mean(r, axis=-1, keepdims=True)
    d = r - mu
    var = jnp.mean(d * d, axis=-1, keepdims=True)
    return d * lax.rsqrt(var + LN_EPS) * g + b


def _wo_ln_kernel(ya_ref, yb_ref, yc_ref, ym_ref, x_ref, wo_ref, g_ref, b_ref, rw_ref, rb_ref,
                  x1_ref, lg_ref):
    merged = ya_ref[...] + yb_ref[...] + yc_ref[...] + ym_ref[...]
    r = DN_ALPHA * x_ref[...] + jnp.dot(merged.astype(BF16), wo_ref[...], preferred_element_type=F32)
    x1 = _layernorm(r, g_ref[...], b_ref[...])
    x1_ref[...] = x1
    lg_ref[...] = jnp.dot(x1, rw_ref[...], preferred_element_type=F32,
                          precision=lax.Precision.HIGHEST) + rb_ref[...]


def _wo_ln(ya, yb, yc, ym, x, wo, g, b, rw, rb, name):
    rows = x.shape[0]
    tm = 128
    row = pl.BlockSpec((tm, D_MODEL), lambda i: (i, 0))

    def full(a):
        return pl.BlockSpec(a.shape, lambda i: (0,) * a.ndim)

    return pl.pallas_call(
        _wo_ln_kernel,
        grid=(rows // tm,),
        in_specs=[row, row, row, row, row, full(wo), full(g), full(b), full(rw), full(rb)],
        out_specs=[row, pl.BlockSpec((tm, LANES), lambda i: (i, 0))],
        out_shape=[jax.ShapeDtypeStruct((rows, D_MODEL), F32),
                   jax.ShapeDtypeStruct((rows, LANES), F32)],
        compiler_params=_cparams(("parallel",), VMEM_LIMIT),
        name=name,
    )(ya, yb, yc, ym, x, wo, g, b, rw, rb)


def _route(logits):
    top_v, top_i = lax.top_k(logits, TOP_K)
    gates = jax.nn.softmax(top_v, axis=-1)
    na = N_TOK * TOP_K
    flat_e = top_i.reshape(-1).astype(jnp.int32)
    order = jnp.argsort(flat_e).astype(jnp.int32)
    se = flat_e[order]
    counts = jnp.bincount(flat_e, length=N_EXP).astype(jnp.int32)
    padded = (counts + MOE_TB - 1) // MOE_TB * MOE_TB
    pend = jnp.cumsum(padded)
    pstart = pend - padded
    ustart = jnp.cumsum(counts) - counts
    dest = pstart[se] + jnp.arange(na, dtype=jnp.int32) - ustart[se]
    row_tok = jnp.zeros((MOE_P,), jnp.int32).at[dest].set(order // TOP_K)
    pos = jnp.zeros((na,), jnp.int32).at[order].set(dest)
    blk_e = jnp.minimum(
        jnp.searchsorted(pend, jnp.arange(MOE_NB, dtype=jnp.int32) * MOE_TB, side='right'),
        N_EXP - 1).astype(jnp.int32)
    n_used = (pend[-1] // MOE_TB).astype(jnp.int32).reshape(1)
    return gates, row_tok, pos, blk_e, n_used


def _row_copy(src_hbm, idx, dst, slot, sem):
    return pltpu.make_async_copy(src_hbm.at[pl.ds(idx, 1)], dst.at[pl.ds(slot, 1)], sem)


def _gather_kernel(nused_ref, tok_ref, x_hbm, o_ref, buf, sem):
    i = pl.program_id(0)

    @pl.when(i < nused_ref[0])
    def _():
        def issue(r, carry):
            _row_copy(x_hbm, tok_ref[r], buf, r, sem).start()
            return carry

        lax.fori_loop(0, MOE_TB, issue, 0)

        def wait(r, carry):
            _row_copy(x_hbm, 0, buf, r, sem).wait()
            return carry

        lax.fori_loop(0, MOE_TB, wait, 0)
        o_ref[...] = buf[...].astype(BF16)

    @pl.when(i >= nused_ref[0])
    def _():
        o_ref[...] = jnp.zeros_like(o_ref)


def _moe_gather(x1, row_tok, n_used):
    return pl.pallas_call(
        _gather_kernel,
        grid_spec=pltpu.PrefetchScalarGridSpec(
            num_scalar_prefetch=1,
            grid=(MOE_NB,),
            in_specs=[pl.BlockSpec((MOE_TB,), lambda i, nu: (i,), memory_space=pltpu.SMEM),
                      pl.BlockSpec(memory_space=pl.ANY)],
            out_specs=pl.BlockSpec((MOE_TB, D_MODEL), lambda i, nu: (i, 0)),
            scratch_shapes=[pltpu.VMEM((MOE_TB, D_MODEL), F32), pltpu.SemaphoreType.DMA(())],
        ),
        out_shape=jax.ShapeDtypeStruct((MOE_P, D_MODEL), BF16),
        compiler_params=_cparams(("arbitrary",)),
        name="moe_gather",
    )(n_used, row_tok, x1)


def _expert_up_kernel(be_ref, nused_ref, x_ref, wg_ref, wu_ref, bg_ref, bu_ref, h_ref):
    i = pl.program_id(1)

    @pl.when(i < nused_ref[0])
    def _():
        x = x_ref[...]
        gate = jnp.dot(x, wg_ref[0], preferred_element_type=F32) + bg_ref[0]
        up = jnp.dot(x, wu_ref[0], preferred_element_type=F32) + bu_ref[0]
        gate = jnp.minimum(gate, SWIGLU_LIMIT)
        up = jnp.clip(up, -SWIGLU_LIMIT, SWIGLU_LIMIT)
        h_ref[...] = ((up + 1.0) * (gate * jax.nn.sigmoid(SWIGLU_ALPHA * gate))).astype(h_ref.dtype)

    @pl.when(i >= nused_ref[0])
    def _():
        h_ref[...] = jnp.zeros_like(h_ref)


def _expert_up(xg, w_gu, b_gu, blk_e, n_used):
    nf = D_FF // MOE_TF

    def used(i, nu):
        return jnp.minimum(i, nu[0] - 1)

    return pl.pallas_call(
        _expert_up_kernel,
        grid_spec=pltpu.PrefetchScalarGridSpec(
            num_scalar_prefetch=2,
            grid=(nf, MOE_NB),
            in_specs=[
                pl.BlockSpec((MOE_TB, D_MODEL), lambda j, i, be, nu: (used(i, nu), 0)),
                pl.BlockSpec((1, D_MODEL, MOE_TF), lambda j, i, be, nu: (be[used(i, nu)], 0, j)),
                pl.BlockSpec((1, D_MODEL, MOE_TF), lambda j, i, be, nu: (be[used(i, nu)], 0, nf + j)),
                pl.BlockSpec((1, 1, MOE_TF), lambda j, i, be, nu: (be[used(i, nu)], 0, j)),
                pl.BlockSpec((1, 1, MOE_TF), lambda j, i, be, nu: (be[used(i, nu)], 0, nf + j)),
            ],
            out_specs=pl.BlockSpec((MOE_TB, MOE_TF), lambda j, i, be, nu: (i, j)),
        ),
        out_shape=jax.ShapeDtypeStruct((MOE_P, D_FF), BF16),
        compiler_params=_cparams(("arbitrary", "arbitrary"), VMEM_LIMIT),
        name="expert_up",
    )(blk_e, n_used, xg, w_gu, w_gu, b_gu, b_gu)


def _expert_down_kernel(be_ref, nused_ref, h_ref, w_ref, b_ref, y_ref):
    i = pl.program_id(0)

    @pl.when(i < nused_ref[0])
    def _():
        y_ref[...] = jnp.dot(h_ref[...], w_ref[0], preferred_element_type=F32) + b_ref[0]

    @pl.when(i >= nused_ref[0])
    def _():
        y_ref[...] = jnp.zeros_like(y_ref)


def _expert_down(h, w_dn, b_dn, blk_e, n_used):
    def used(i, nu):
        return jnp.minimum(i, nu[0] - 1)

    return pl.pallas_call(
        _expert_down_kernel,
        grid_spec=pltpu.PrefetchScalarGridSpec(
            num_scalar_prefetch=2,
            grid=(MOE_NB,),
            in_specs=[
                pl.BlockSpec((MOE_TB, D_FF), lambda i, be, nu: (used(i, nu), 0)),
                pl.BlockSpec((1, D_FF, D_MODEL), lambda i, be, nu: (be[used(i, nu)], 0, 0)),
                pl.BlockSpec((1, 1, D_MODEL), lambda i, be, nu: (be[used(i, nu)], 0, 0)),
            ],
            out_specs=pl.BlockSpec((MOE_TB, D_MODEL), lambda i, be, nu: (i, 0)),
        ),
        out_shape=jax.ShapeDtypeStruct((MOE_P, D_MODEL), F32),
        compiler_params=_cparams(("arbitrary",), VMEM_LIMIT),
        name="expert_down",
    )(blk_e, n_used, h, w_dn, b_dn)


def _combine_kernel(pos_ref, gates_ref, x1_ref, y_hbm, g_ref, b_ref, x2_ref, x2b_ref, buf, sem):
    def issue(a, carry):
        _row_copy(y_hbm, pos_ref[a], buf, a, sem).start()
        return carry

    lax.fori_loop(0, CMB_T * TOP_K, issue, 0)

    def wait(a, carry):
        _row_copy(y_hbm, 0, buf, a, sem).wait()
        return carry

    lax.fori_loop(0, CMB_T * TOP_K, wait, 0)
    gates = gates_ref[...]
    y = jnp.zeros((CMB_T, D_MODEL), F32)
    for k in range(TOP_K):
        y = y + buf[k * CMB_T:(k + 1) * CMB_T, :] * gates[:, k:k + 1]
    x2 = _layernorm(DN_ALPHA * x1_ref[...] + y, g_ref[...], b_ref[...])
    x2_ref[...] = x2
    x2b_ref[...] = x2.astype(BF16)


def _moe_combine(pos_kmajor, gates, x1, y, g, b):
    row = pl.BlockSpec((CMB_T, D_MODEL), lambda i: (i, 0))
    vec = pl.BlockSpec((1, D_MODEL), lambda i: (0, 0))
    return pl.pallas_call(
        _combine_kernel,
        grid=(N_TOK // CMB_T,),
        in_specs=[pl.BlockSpec((CMB_T * TOP_K,), lambda i: (i,), memory_space=pltpu.SMEM),
                  pl.BlockSpec((CMB_T, TOP_K), lambda i: (i, 0)),
                  row, pl.BlockSpec(memory_space=pl.ANY), vec, vec],
        out_specs=[row, row],
        out_shape=[jax.ShapeDtypeStruct((N_TOK, D_MODEL), F32),
                   jax.ShapeDtypeStruct((N_TOK, D_MODEL), BF16)],
        scratch_shapes=[pltpu.VMEM((CMB_T * TOP_K, D_MODEL), F32), pltpu.SemaphoreType.DMA(())],
        compiler_params=_cparams(("arbitrary",)),
        name="moe_combine",
    )(pos_kmajor, gates, x1, y, g, b)


def _permute_w_in(w):
    parts, off = [], 0
    for n in REF_IN_SIZES:
        parts.append((off, n))
        off += n
    return jnp.concatenate([w[:, parts[i][0]:parts[i][0] + parts[i][1]] for i in MY_ORDER],
                           axis=1).astype(BF16)


def kernel(x_prompt, x_sample, cache_conv, state_ret, cache_swa_k, cache_swa_v, cache_mem_k, cache_mem_v, mem_prompt, w_in, conv_w, ret_gn_w, attn_sinks, w_mem_kv, w_o, ln1_g, ln1_b, router_w, router_b, w_gate_up, b_gate_up, w_down, b_down, ln2_g, ln2_b):
    pos_p = jnp.arange(SEQ, dtype=jnp.int32)
    pos_s = PAST_LEN + jnp.arange(DEC_SEQ, dtype=jnp.int32)
    ret_tabs_p = _ret_tables(pos_p, R_CHUNK)
    ret_tabs_s = _ret_tables(pos_s, DEC_SEQ)
    swa_tabs_p = _swa_tables(pos_p)
    swa_tabs_s = _swa_tables(pos_s)
    kvw = A_KV * A_HD

    xp = x_prompt.reshape(N_PROMPT, D_MODEL)
    xs = x_sample.reshape(N_SAMPLE, D_MODEL)
    xb = jnp.concatenate([xp, xs], axis=0).astype(BF16)
    mem_b = mem_prompt.reshape(BATCH * N_MEM, D_MODEL).astype(BF16)
    zeros_e = jnp.zeros((256, D_MODEL), F32)

    outs = {k: [] for k in ("p_conv", "p_ret", "p_k", "p_v", "p_mk", "p_mv",
                            "s_conv", "s_ret", "s_k", "s_v")}
    for l in range(DEPTH):
        z = _matmul(xb, _permute_w_in(w_in[l]), 1536, 512)
        gn = ret_gn_w[l].reshape(1, D_MODEL)
        sinks = attn_sinks[l]

        ya_p, tail_p = _conv_branch(z, conv_w[l], zeros_e, zeros_e, 0, N_PROMPT, SEQ, False)
        yb_p, ret_p = _ret_prompt(z, ret_tabs_p, gn)
        yc_p, kr_p = _swa_prompt(z, sinks, swa_tabs_p)
        mkv = _matmul(mem_b, w_mem_kv[l].astype(BF16), 512, 512)
        mk_p = mkv[:, :D_MODEL].reshape(BATCH, N_MEM, D_MODEL)
        mv_p = mkv[:, D_MODEL:].reshape(BATCH, N_MEM, D_MODEL)
        ym_p = _mem_attend(z, mk_p, mv_p, 0, SEQ, 512, "mem_prompt")

        ea = jnp.repeat(cache_conv[l][:, 0], DEC_SEQ, axis=0)
        eb = jnp.repeat(cache_conv[l][:, 1], DEC_SEQ, axis=0)
        ya_s, u_s = _conv_branch(z, conv_w[l], ea, eb, N_PROMPT, N_SAMPLE, DEC_SEQ, True)
        yb_s, ret_s = _ret_sample(z, ret_tabs_s, gn, state_ret[l])
        yc_s, nk_s, nv_s = _swa_sample(z, sinks, swa_tabs_s,
                                       cache_swa_k[l].reshape(DEC_BATCH, WINDOW, kvw),
                                       cache_swa_v[l].reshape(DEC_BATCH, WINDOW, kvw))
        ym_s = _mem_attend(z, cache_mem_k[l].reshape(DEC_BATCH, N_MEM, D_MODEL),
                           cache_mem_v[l].reshape(DEC_BATCH, N_MEM, D_MODEL),
                           N_PROMPT, DEC_SEQ, DEC_SEQ, "mem_sample")

        wo_b = w_o[l].astype(BF16)
        g1 = ln1_g[l].reshape(1, D_MODEL)
        b1 = ln1_b[l].reshape(1, D_MODEL)
        rw = jnp.pad(router_w[l], ((0, 0), (0, LANES - N_EXP)))
        rb = jnp.pad(router_b[l], (0, LANES - N_EXP)).reshape(1, LANES)
        x1_p, lg_p = _wo_ln(ya_p, yb_p, yc_p, ym_p, xp, wo_b, g1, b1, rw, rb, "wo_ln_prompt")
        x1_s, lg_s = _wo_ln(ya_s, yb_s, yc_s, ym_s, xs, wo_b, g1, b1, rw, rb, "wo_ln_sample")
        x1 = jnp.concatenate([x1_p, x1_s], axis=0)
        logits = jnp.concatenate([lg_p, lg_s], axis=0)[:, :N_EXP]

        gates, row_tok, pos, blk_e, n_used = _route(logits)
        xg = _moe_gather(x1, row_tok, n_used)
        h = _expert_up(xg, w_gate_up[l].astype(BF16), b_gate_up[l].reshape(N_EXP, 1, 2 * D_FF),
                       blk_e, n_used)
        y = _expert_down(h, w_down[l].astype(BF16), b_down[l].reshape(N_EXP, 1, D_MODEL),
                         blk_e, n_used)
        pos_k = pos.reshape(N_TOK // CMB_T, CMB_T, TOP_K).transpose(0, 2, 1).reshape(-1)
        x2, xb = _moe_combine(pos_k, gates, x1, y, ln2_g[l].reshape(1, D_MODEL),
                              ln2_b[l].reshape(1, D_MODEL))
        xp, xs = x2[:N_PROMPT], x2[N_PROMPT:]

        tail_p = tail_p.reshape(BATCH, SEQ // 256, SUBLANES, D_MODEL)
        outs["p_conv"].append(tail_p[:, -1, -(CONV_W - 1):])
        outs["p_ret"].append(ret_p)
        outs["p_k"].append(kr_p.reshape(BATCH, SEQ, A_KV, A_HD)[:, -WINDOW:])
        outs["p_v"].append(z[:N_PROMPT, AV:AV + kvw].reshape(BATCH, SEQ, A_KV, A_HD)[:, -WINDOW:])
        outs["p_mk"].append(mk_p.reshape(BATCH, N_MEM, M_HEADS, M_HD))
        outs["p_mv"].append(mv_p.reshape(BATCH, N_MEM, M_HEADS, M_HD))
        outs["s_conv"].append(u_s.reshape(DEC_BATCH, DEC_SEQ, D_MODEL)[:, -(CONV_W - 1):])
        outs["s_ret"].append(ret_s)
        outs["s_k"].append(nk_s.reshape(DEC_BATCH, WINDOW, A_KV, A_HD))
        outs["s_v"].append(nv_s.reshape(DEC_BATCH, WINDOW, A_KV, A_HD))

    return (xp.reshape(BATCH, SEQ, D_MODEL), xs.reshape(DEC_BATCH, DEC_SEQ, D_MODEL),
            jnp.stack(outs["p_conv"]), jnp.stack(outs["p_ret"]), jnp.stack(outs["p_k"]),
            jnp.stack(outs["p_v"]), jnp.stack(outs["p_mk"]), jnp.stack(outs["p_mv"]),
            jnp.stack(outs["s_conv"]), jnp.stack(outs["s_ret"]), jnp.stack(outs["s_k"]),
            jnp.stack(outs["s_v"]))
```

```python
import functools

import jax
import jax.numpy as jnp
from jax import lax
from jax.experimental import pallas as pl
from jax.experimental.pallas import tpu as pltpu

F32 = jnp.float32
BF16 = jnp.bfloat16

D_MODEL = 2048
BATCH = 2
SEQ = 4096
DEPTH = 2
DEC_BATCH = 128
DEC_SEQ = 8
PAST_LEN = 8192
N_PROMPT = BATCH * SEQ
N_SAMPLE = DEC_BATCH * DEC_SEQ
N_TOK = N_PROMPT + N_SAMPLE

CONV_W = 3
R_HEADS = 8
R_DK = 128
R_DV = D_MODEL // R_HEADS
R_CHUNK = 128
R_THETA = 10000.0
A_HEADS = 32
A_KV = 4
A_HD = D_MODEL // A_HEADS
A_GROUP = A_HEADS // A_KV
WINDOW = 128
ROPE_DIMS = A_HD // 4
ROPE_THETA = 500000.0
N_MEM = 256
M_HEADS = 4
M_HD = D_MODEL // M_HEADS
N_EXP = 32
TOP_K = 4
D_FF = D_MODEL
SWIGLU_LIMIT = 7.0
SWIGLU_ALPHA = 1.702
LN_EPS = 1e-5
GN_EPS = 1e-6
DN_ALPHA = (2 * DEPTH) ** 0.25

REF_IN_SIZES = (D_MODEL, D_MODEL, D_MODEL, R_HEADS * R_DK, R_HEADS * R_DK, D_MODEL, D_MODEL,
                D_MODEL, A_KV * A_HD, A_KV * A_HD, D_MODEL, 4 * D_MODEL)
D_IN = sum(REF_IN_SIZES)
MY_ORDER = (11, 0, 1, 2, 5, 6, 7, 10, 3, 4, 8, 9)
GL, HC, BC, CC, RV, RG, AQ, MQ, RQ, RK, AK, AV = (
    0, 8192, 10240, 12288, 14336, 16384, 18432, 20480, 22528, 23552, 24576, 24832)

LANES = 128
SUBLANES = 8
VMEM_LIMIT = 56 * 1024 * 1024

MOE_TB = 256
MOE_NB = N_TOK * TOP_K // MOE_TB + N_EXP
MOE_P = MOE_NB * MOE_TB
MOE_TF = 1024
CMB_T = 64


def _cparams(sem, vmem=None):
    return pltpu.CompilerParams(dimension_semantics=sem, vmem_limit_bytes=vmem)


def _mm_kernel(x_ref, w_ref, o_ref):
    o_ref[...] = jnp.dot(x_ref[...], w_ref[...], preferred_element_type=F32).astype(o_ref.dtype)


def _matmul(x, w, tm, tn, out_dtype=F32):
    m, k = x.shape
    n = w.shape[1]
    return pl.pallas_call(
        _mm_kernel,
        grid=(m // tm, n // tn),
        in_specs=[pl.BlockSpec((tm, k), lambda i, j: (i, 0)),
                  pl.BlockSpec((k, tn), lambda i, j: (0, j))],
        out_specs=pl.BlockSpec((tm, tn), lambda i, j: (i, j)),
        out_shape=jax.ShapeDtypeStruct((m, n), out_dtype),
        compiler_params=_cparams(("parallel", "parallel"), VMEM_LIMIT),
        name="matmul",
    )(x, w)


def _conv_kernel(seq_len, full_u, hc_ref, cc_ref, bc_ref, g_ref, hh_ref, ch_ref, ea_ref, eb_ref,
                 w_ref, ya_ref, tail_ref):
    i = pl.program_id(0)
    tm = hc_ref.shape[0]
    u = cc_ref[...] * hc_ref[...]
    ue = jnp.concatenate([ch_ref[...] * hh_ref[...], u], axis=0)
    p1 = pltpu.roll(ue, 1, 0)[SUBLANES:]
    p2 = pltpu.roll(ue, 2, 0)[SUBLANES:]
    t = (i * tm + lax.broadcasted_iota(jnp.int32, (tm, 1), 0)) % seq_len
    ea = ea_ref[...]
    eb = eb_ref[...]
    p1 = jnp.where(t == 0, eb, p1)
    p2 = jnp.where(t == 0, ea, jnp.where(t == 1, eb, p2))
    w = w_ref[...]
    y = p2 * w[0:1] + p1 * w[1:2] + u * w[2:3]
    ya_ref[...] = jax.nn.sigmoid(g_ref[...]) * (bc_ref[...] * y)
    tail_ref[...] = u if full_u else u[tm - SUBLANES:]


def _conv_branch(z, conv_w, ea, eb, row0, rows, seq_len, sample):
    tm, tc = 256, 512
    r0 = row0 // tm
    nj = D_MODEL // tc

    def zspec(col):
        return pl.BlockSpec((tm, tc), lambda i, j: (r0 + i, col // tc + j))

    def halo(col):
        return pl.BlockSpec(
            (SUBLANES, tc),
            lambda i, j: (jnp.maximum((r0 + i) * (tm // SUBLANES) - 1, 0), col // tc + j))

    if sample:
        espec = pl.BlockSpec((tm, tc), lambda i, j: (i, j))
        tail_spec = pl.BlockSpec((tm, tc), lambda i, j: (i, j))
        tail_shape = jax.ShapeDtypeStruct((rows, D_MODEL), F32)
    else:
        espec = pl.BlockSpec((tm, tc), lambda i, j: (0, j))
        tail_spec = pl.BlockSpec((SUBLANES, tc), lambda i, j: (i, j))
        tail_shape = jax.ShapeDtypeStruct((rows // tm * SUBLANES, D_MODEL), F32)
    return pl.pallas_call(
        functools.partial(_conv_kernel, seq_len, sample),
        grid=(rows // tm, nj),
        in_specs=[zspec(HC), zspec(CC), zspec(BC), zspec(GL), halo(HC), halo(CC), espec, espec,
                  pl.BlockSpec((CONV_W, tc), lambda i, j: (0, j))],
        out_specs=[pl.BlockSpec((tm, tc), lambda i, j: (i, j)), tail_spec],
        out_shape=[jax.ShapeDtypeStruct((rows, D_MODEL), F32), tail_shape],
        compiler_params=_cparams(("parallel", "parallel"), VMEM_LIMIT),
        name="conv_sample" if sample else "conv_prompt",
    )(z, z, z, z, z, z, ea, eb, conv_w)


def _rope_full(x, cos, sin):
    return x * cos + pltpu.roll(x, R_DK // 2, 1) * sin


def _bf16_exact(x):
    return x.astype(BF16).astype(F32)


def _group_norm_gate(o, gn, rg, g):
    mu = jnp.mean(o, axis=-1, keepdims=True)
    d = o - mu
    var = jnp.mean(d * d, axis=-1, keepdims=True)
    yn = d * lax.rsqrt(var + GN_EPS) * gn
    return jax.nn.sigmoid(g) * ((rg * jax.nn.sigmoid(rg)) * yn)


def _ret_prompt_kernel(q_ref, k_ref, v_ref, rg_ref, g_ref, cos_ref, sin_ref, dmask_ref, qdec_ref,
                       kdec_ref, cdec_ref, gn_ref, y_ref, sfin_ref, s_scr):
    c = pl.program_id(1)

    @pl.when(c == 0)
    def _():
        s_scr[...] = jnp.zeros_like(s_scr)

    cos = cos_ref[...]
    sin = sin_ref[...]
    for h in range(R_HEADS):
        qs = slice(h * R_DK, (h + 1) * R_DK)
        vs = slice(h * R_DV, (h + 1) * R_DV)
        q = _rope_full(q_ref[:, qs], cos, sin)
        k = _rope_full(k_ref[:, qs], cos, sin) * (R_DK ** -0.5)
        vb = v_ref[:, vs].astype(BF16)
        s = s_scr[h]
        att = lax.dot_general(q.astype(BF16), k.astype(BF16), (((1,), (1,)), ((), ())),
                              preferred_element_type=F32) * dmask_ref[h]
        o = (jnp.dot(att.astype(BF16), vb, preferred_element_type=F32)
             + jnp.dot((q * qdec_ref[h]).astype(BF16), s.astype(BF16), preferred_element_type=F32))
        kd = (k * kdec_ref[h]).astype(BF16)
        s_new = cdec_ref[h] * s + lax.dot_general(kd, vb, (((0,), (0,)), ((), ())),
                                                  preferred_element_type=F32)
        s_scr[h] = s_new
        sfin_ref[0, h] = s_new
        y_ref[:, vs] = _group_norm_gate(o, gn_ref[:, vs], rg_ref[:, vs], g_ref[:, vs])


def _ret_prompt(z, tabs, gn_w):
    cos, sin, dmask, qdec, kdec, cdec = tabs
    nc = SEQ // R_CHUNK

    def zspec(col, width):
        return pl.BlockSpec((R_CHUNK, width), lambda b, c: (b * nc + c, col // width))

    def full(a):
        return pl.BlockSpec(a.shape, lambda b, c: (0,) * a.ndim)

    tspec = pl.BlockSpec((R_CHUNK, R_DK), lambda b, c: (c, 0))
    return pl.pallas_call(
        _ret_prompt_kernel,
        grid=(BATCH, nc),
        in_specs=[zspec(RQ, R_HEADS * R_DK), zspec(RK, R_HEADS * R_DK), zspec(RV, D_MODEL),
                  zspec(RG, D_MODEL), zspec(GL + D_MODEL, D_MODEL), tspec, tspec,
                  full(dmask), full(qdec), full(kdec), full(cdec), full(gn_w)],
        out_specs=[pl.BlockSpec((R_CHUNK, D_MODEL), lambda b, c: (b * nc + c, 0)),
                   pl.BlockSpec((1, R_HEADS, R_DK, R_DV), lambda b, c: (b, 0, 0, 0))],
        out_shape=[jax.ShapeDtypeStruct((N_PROMPT, D_MODEL), F32),
                   jax.ShapeDtypeStruct((BATCH, R_HEADS, R_DK, R_DV), F32)],
        scratch_shapes=[pltpu.VMEM((R_HEADS, R_DK, R_DV), F32)],
        compiler_params=_cparams(("parallel", "arbitrary"), VMEM_LIMIT),
        name="retention_prompt",
    )(z, z, z, z, z, cos, sin, dmask, qdec, kdec, cdec, gn_w)


def _ret_sample_kernel(q_ref, k_ref, v_ref, rg_ref, g_ref, cos_ref, sin_ref, dmask_ref, qdec_ref,
                       kdec_ref, cdec_ref, gn_ref, s_ref, y_ref, snew_ref):
    cos = cos_ref[...]
    sin = sin_ref[...]
    for h in range(R_HEADS):
        qs = slice(h * R_DK, (h + 1) * R_DK)
        vs = slice(h * R_DV, (h + 1) * R_DV)
        q = _rope_full(q_ref[:, qs], cos, sin)
        k = _rope_full(k_ref[:, qs], cos, sin) * (R_DK ** -0.5)
        v = _bf16_exact(v_ref[:, vs])
        s = s_ref[0, 0, h]
        att = lax.dot_general(_bf16_exact(q), _bf16_exact(k), (((1,), (1,)), ((), ())),
                              preferred_element_type=F32) * dmask_ref[h]
        o = (jnp.dot(_bf16_exact(att), v, preferred_element_type=F32)
             + jnp.dot((q * qdec_ref[h]).astype(BF16), s.astype(BF16), preferred_element_type=F32))
        kd = _bf16_exact(k * kdec_ref[h])
        snew_ref[0, h] = cdec_ref[h] * s + lax.dot_general(
            kd, v, (((0,), (0,)), ((), ())), preferred_element_type=F32)
        y_ref[:, vs] = _group_norm_gate(o, gn_ref[:, vs], rg_ref[:, vs], g_ref[:, vs])


def _ret_sample(z, tabs, gn_w, state, layer):
    cos, sin, dmask, qdec, kdec, cdec = tabs
    r0 = N_PROMPT // DEC_SEQ

    def zspec(col, width):
        return pl.BlockSpec((DEC_SEQ, width), lambda b: (r0 + b, col // width))

    def full(a):
        return pl.BlockSpec(a.shape, lambda b: (0,) * a.ndim)

    sspec = pl.BlockSpec((1, R_HEADS, R_DK, R_DV), lambda b: (b, 0, 0, 0))
    sin_spec = pl.BlockSpec((1, 1, R_HEADS, R_DK, R_DV), lambda b: (layer, b, 0, 0, 0))
    return pl.pallas_call(
        _ret_sample_kernel,
        grid=(DEC_BATCH,),
        in_specs=[zspec(RQ, R_HEADS * R_DK), zspec(RK, R_HEADS * R_DK), zspec(RV, D_MODEL),
                  zspec(RG, D_MODEL), zspec(GL + D_MODEL, D_MODEL),
                  full(cos), full(sin), full(dmask), full(qdec), full(kdec), full(cdec), full(gn_w),
                  sin_spec],
        out_specs=[pl.BlockSpec((DEC_SEQ, D_MODEL), lambda b: (b, 0)), sspec],
        out_shape=[jax.ShapeDtypeStruct((N_SAMPLE, D_MODEL), F32),
                   jax.ShapeDtypeStruct(state.shape[1:], F32)],
        compiler_params=_cparams(("parallel",)),
        name="retention_sample",
    )(z, z, z, z, z, cos, sin, dmask, qdec, kdec, cdec, gn_w, state)


def _ret_tables(pos, chunk):
    half = R_DK // 2
    inv = R_THETA ** (-jnp.arange(half, dtype=F32) / half)
    ang = pos.astype(F32)[:, None] * inv[None, :]
    cos = jnp.concatenate([jnp.cos(ang), jnp.cos(ang)], axis=-1)
    sin = jnp.concatenate([-jnp.sin(ang), jnp.sin(ang)], axis=-1)
    lg = jnp.log(1.0 - 2.0 ** (-5.0 - jnp.arange(R_HEADS, dtype=F32)))
    i = jnp.arange(chunk, dtype=F32)
    diff = i[:, None] - i[None, :]
    dmask = jnp.where(diff >= 0, jnp.exp(lg[:, None, None] * jnp.maximum(diff, 0.0)), 0.0)
    qdec = jnp.broadcast_to(jnp.exp(lg[:, None] * (i + 1.0))[:, :, None], (R_HEADS, chunk, R_DK))
    kdec = jnp.broadcast_to(jnp.exp(lg[:, None] * (chunk - 1.0 - i))[:, :, None],
                            (R_HEADS, chunk, R_DK))
    cdec = jnp.broadcast_to(jnp.exp(lg * chunk)[:, None, None], (R_HEADS, 1, R_DV))
    return cos, sin, dmask.astype(F32), qdec.astype(F32), kdec.astype(F32), cdec.astype(F32)


def _swa_tables(pos):
    half = ROPE_DIMS // 2
    inv = ROPE_THETA ** (-jnp.arange(half, dtype=F32) / half)
    ang = pos.astype(F32)[:, None] * inv[None, :]
    cos, sin = jnp.cos(ang), jnp.sin(ang)
    t = pos.shape[0]
    ones = jnp.ones((t, A_HD - ROPE_DIMS), F32)
    zeros = jnp.zeros((t, A_HD - ROPE_DIMS), F32)
    zh = jnp.zeros((t, half), F32)
    c = jnp.concatenate([cos, cos, ones], axis=-1)
    s_lo = jnp.concatenate([-sin, zh, zeros], axis=-1)
    s_hi = jnp.concatenate([zh, sin, zeros], axis=-1)
    rep = LANES // A_HD
    return jnp.tile(c, (1, rep)), jnp.tile(s_lo, (1, rep)), jnp.tile(s_hi, (1, rep))


def _rope_partial(x, c, s_lo, s_hi):
    half = ROPE_DIMS // 2
    return x * c + pltpu.roll(x, LANES - half, 1) * s_lo + pltpu.roll(x, half, 1) * s_hi


def _dup_head(chunk, odd):
    lane = lax.broadcasted_iota(jnp.int32, chunk.shape, 1)
    sw = pltpu.roll(chunk, A_HD, 1)
    if odd:
        return jnp.where(lane < A_HD, sw, chunk)
    return jnp.where(lane < A_HD, chunk, sw)


def _swa_core(sink_ref, q_ref, g_ref, y_ref, kp, kc, vp, vc, tabs_q, has_prev, tq, chunks_per_dot):
    cq, slq, shq = tabs_q
    lane_q = lax.broadcasted_iota(jnp.int32, (tq, LANES), 1)
    n_chunks = A_GROUP * A_HD // LANES
    for h in range(A_KV):
        kchunk = slice((h // 2) * LANES, (h // 2 + 1) * LANES)
        kkp = _dup_head(kp[:, kchunk], h % 2).astype(BF16)
        kkc = _dup_head(kc[:, kchunk], h % 2).astype(BF16)
        vvp = _dup_head(vp[:, kchunk], h % 2).astype(BF16)
        vvc = _dup_head(vc[:, kchunk], h % 2).astype(BF16)
        for c0 in range(0, n_chunks, chunks_per_dot):
            pieces, sinks = [], []
            for ci in range(c0, c0 + chunks_per_dot):
                col = h * A_GROUP * A_HD + ci * LANES
                qc = _rope_partial(q_ref[:, col:col + LANES], cq, slq, shq) * (A_HD ** -0.5)
                for half in range(2):
                    keep = (lane_q < A_HD) if half == 0 else (lane_q >= A_HD)
                    pieces.append(jnp.where(keep, qc, 0.0).astype(BF16))
                    head = h * A_GROUP + 2 * ci + half
                    sinks.append(jnp.full((tq, 1), sink_ref[head], F32))
            qs = jnp.concatenate(pieces, axis=0)
            sink = jnp.concatenate(sinks, axis=0)
            r = qs.shape[0]
            nt = (((1,), (1,)), ((), ()))
            sp = lax.dot_general(qs, kkp, nt, preferred_element_type=F32)
            sc = lax.dot_general(qs, kkc, nt, preferred_element_type=F32)
            t = lax.broadcasted_iota(jnp.int32, (r, WINDOW), 0) % tq
            j = lax.broadcasted_iota(jnp.int32, (r, WINDOW), 1)
            sp = jnp.where((j >= t) & has_prev, sp, -jnp.inf)
            sc = jnp.where(j <= t, sc, -jnp.inf)
            m = jnp.maximum(jnp.maximum(jnp.max(sp, axis=-1, keepdims=True),
                                        jnp.max(sc, axis=-1, keepdims=True)), sink)
            pp = jnp.exp(sp - m)
            pc = jnp.exp(sc - m)
            den = (jnp.sum(pp, axis=-1, keepdims=True) + jnp.sum(pc, axis=-1, keepdims=True)
                   + jnp.exp(sink - m))
            inv = 1.0 / den
            o = (jnp.dot((pp * inv).astype(BF16), vvp, preferred_element_type=F32)
                 + jnp.dot((pc * inv).astype(BF16), vvc, preferred_element_type=F32))
            for n, ci in enumerate(range(c0, c0 + chunks_per_dot)):
                col = h * A_GROUP * A_HD + ci * LANES
                oa = o[(2 * n) * tq:(2 * n + 1) * tq]
                ob = o[(2 * n + 1) * tq:(2 * n + 2) * tq]
                oc = jnp.where(lane_q < A_HD, oa, ob)
                y_ref[:, col:col + LANES] = jax.nn.sigmoid(g_ref[:, col:col + LANES]) * oc


def _rope_kv(k_ref, ck, slk, shk):
    parts = [_rope_partial(k_ref[:, c * LANES:(c + 1) * LANES], ck, slk, shk)
             for c in range(A_KV * A_HD // LANES)]
    return jnp.concatenate(parts, axis=-1)


def _swa_prompt_kernel(sink_ref, q_ref, k_ref, v_ref, g_ref, c_ref, sl_ref, sh_ref,
                       y_ref, kr_ref, kp_scr, vp_scr):
    n = pl.program_id(1)

    @pl.when(n == 0)
    def _():
        kp_scr[...] = jnp.zeros_like(kp_scr)
        vp_scr[...] = jnp.zeros_like(vp_scr)

    tabs = (c_ref[...], sl_ref[...], sh_ref[...])
    kc = _rope_kv(k_ref, *tabs)
    vc = v_ref[...]
    kr_ref[...] = kc
    _swa_core(sink_ref, q_ref, g_ref, y_ref, kp_scr[...], kc, vp_scr[...], vc, tabs, n > 0,
              WINDOW, 1)
    kp_scr[...] = kc
    vp_scr[...] = vc


def _swa_prompt(z, sinks, tabs):
    nb = SEQ // WINDOW
    kvw = A_KV * A_HD

    def zspec(col, width):
        return pl.BlockSpec((WINDOW, width), lambda b, n: (b * nb + n, col // width))

    tspec = pl.BlockSpec((WINDOW, LANES), lambda b, n: (n, 0))
    return pl.pallas_call(
        _swa_prompt_kernel,
        grid=(BATCH, nb),
        in_specs=[pl.BlockSpec(memory_space=pltpu.SMEM),
                  zspec(AQ, D_MODEL), zspec(AK, kvw), zspec(AV, kvw), zspec(GL + 2 * D_MODEL, D_MODEL),
                  tspec, tspec, tspec],
        out_specs=[pl.BlockSpec((WINDOW, D_MODEL), lambda b, n: (b * nb + n, 0)),
                   pl.BlockSpec((WINDOW, kvw), lambda b, n: (b * nb + n, 0))],
        out_shape=[jax.ShapeDtypeStruct((N_PROMPT, D_MODEL), F32),
                   jax.ShapeDtypeStruct((N_PROMPT, kvw), F32)],
        scratch_shapes=[pltpu.VMEM((WINDOW, kvw), F32), pltpu.VMEM((WINDOW, kvw), F32)],
        compiler_params=_cparams(("parallel", "arbitrary"), VMEM_LIMIT),
        name="swa_prompt",
    )(sinks, z, z, z, z, *tabs)


def _swa_sample_kernel(sink_ref, q_ref, k_ref, v_ref, g_ref, c_ref, sl_ref, sh_ref, ck_ref, cv_ref,
                       y_ref, nk_ref, nv_ref):
    tabs = (c_ref[...], sl_ref[...], sh_ref[...])
    kn = _rope_kv(k_ref, *tabs)
    vn = v_ref[...]
    pad = jnp.zeros((WINDOW - DEC_SEQ, A_KV * A_HD), F32)
    kp = ck_ref[0, 0]
    vp = cv_ref[0, 0]
    _swa_core(sink_ref, q_ref, g_ref, y_ref, kp, jnp.concatenate([kn, pad], axis=0),
              vp, jnp.concatenate([vn, pad], axis=0), tabs, True, DEC_SEQ,
              A_GROUP * A_HD // LANES)
    nk_ref[0] = jnp.concatenate([kp[DEC_SEQ:], kn], axis=0)
    nv_ref[0] = jnp.concatenate([vp[DEC_SEQ:], vn], axis=0)


def _swa_sample(z, sinks, tabs, cache_k, cache_v, layer):
    kvw = A_KV * A_HD
    r0 = N_PROMPT // DEC_SEQ

    def zspec(col, width):
        return pl.BlockSpec((DEC_SEQ, width), lambda b: (r0 + b, col // width))

    tspec = pl.BlockSpec((DEC_SEQ, LANES), lambda b: (0, 0))
    cspec = pl.BlockSpec((1, WINDOW, kvw), lambda b: (b, 0, 0))
    cin_spec = pl.BlockSpec((1, 1, WINDOW, kvw), lambda b: (layer, b, 0, 0))
    return pl.pallas_call(
        _swa_sample_kernel,
        grid=(DEC_BATCH,),
        in_specs=[pl.BlockSpec(memory_space=pltpu.SMEM),
                  zspec(AQ, D_MODEL), zspec(AK, kvw), zspec(AV, kvw), zspec(GL + 2 * D_MODEL, D_MODEL),
                  tspec, tspec, tspec, cin_spec, cin_spec],
        out_specs=[pl.BlockSpec((DEC_SEQ, D_MODEL), lambda b: (b, 0)), cspec, cspec],
        out_shape=[jax.ShapeDtypeStruct((N_SAMPLE, D_MODEL), F32),
                   jax.ShapeDtypeStruct((DEC_BATCH, WINDOW, kvw), F32),
                   jax.ShapeDtypeStruct((DEC_BATCH, WINDOW, kvw), F32)],
        compiler_params=_cparams(("parallel",)),
        name="swa_sample",
    )(sinks, z, z, z, z, *tabs, cache_k, cache_v)


def _mem_kernel(native, q_ref, k_ref, v_ref, g_ref, y_ref):
    for h in range(M_HEADS):
        hs = slice(h * M_HD, (h + 1) * M_HD)
        q = q_ref[:, hs].astype(BF16)
        if native:
            k = k_ref[0, 0, :, h, :].astype(BF16)
            v = v_ref[0, 0, :, h, :].astype(BF16)
        else:
            k = k_ref[0, :, hs].astype(BF16)
            v = v_ref[0, :, hs].astype(BF16)
        s = lax.dot_general(q, k, (((1,), (1,)), ((), ())), preferred_element_type=F32) * (M_HD ** -0.5)
        m = jnp.max(s, axis=-1, keepdims=True)
        p = jnp.exp(s - m)
        p = p / jnp.sum(p, axis=-1, keepdims=True)
        o = jnp.dot(p.astype(BF16), v, preferred_element_type=F32)
        y_ref[:, hs] = jax.nn.sigmoid(g_ref[:, hs]) * o


def _mem_attend(z, mk, mv, row0, nb, rows_per_b, tq, name, layer=None):
    nq = rows_per_b // tq
    r0 = row0 // tq

    def zspec(col):
        return pl.BlockSpec((tq, D_MODEL), lambda b, i: (r0 + b * nq + i, col // D_MODEL))

    if layer is None:
        kvspec = pl.BlockSpec((1, N_MEM, D_MODEL), lambda b, i: (b, 0, 0))
    else:
        kvspec = pl.BlockSpec((1, 1, N_MEM, M_HEADS, M_HD), lambda b, i: (layer, b, 0, 0, 0))
    return pl.pallas_call(
        functools.partial(_mem_kernel, layer is not None),
        grid=(nb, nq),
        in_specs=[zspec(MQ), kvspec, kvspec, zspec(GL + 3 * D_MODEL)],
        out_specs=pl.BlockSpec((tq, D_MODEL), lambda b, i: (b * nq + i, 0)),
        out_shape=jax.ShapeDtypeStruct((nb * rows_per_b, D_MODEL), F32),
        compiler_params=_cparams(("parallel", "parallel"), VMEM_LIMIT),
        name=name,
    )(z, mk, mv, z)


def _layernorm(r, g, b):
    mu = jnp.mean(r, axis=-1, keepdims=True)
    d = r - mu
    var = jnp.mean(d * d, axis=-1, keepdims=True)
    return d * lax.rsqrt(var + LN_EPS) * g + b


def _wo_ln_kernel(ya_ref, yb_ref, yc_ref, ym_ref, x_ref, wo_ref, g_ref, b_ref, rw_ref, rb_ref,
                  x1_ref, lg_ref):
    merged = ya_ref[...] + yb_ref[...] + yc_ref[...] + ym_ref[...]
    r = DN_ALPHA * x_ref[...] + jnp.dot(merged.astype(BF16), wo_ref[...], preferred_element_type=F32)
    x1 = _layernorm(r, g_ref[...], b_ref[...])
    x1_ref[...] = x1
    lg_ref[...] = jnp.dot(x1, rw_ref[...], preferred_element_type=F32,
                          precision=lax.Precision.HIGHEST) + rb_ref[...]


def _wo_ln(ya, yb, yc, ym, x, wo, g, b, rw, rb, name):
    rows = x.shape[0]
    tm = 128
    row = pl.BlockSpec((tm, D_MODEL), lambda i: (i, 0))

    def full(a):
        return pl.BlockSpec(a.shape, lambda i: (0,) * a.ndim)

    return pl.pallas_call(
        _wo_ln_kernel,
        grid=(rows // tm,),
        in_specs=[row, row, row, row, row, full(wo), full(g), full(b), full(rw), full(rb)],
        out_specs=[row, pl.BlockSpec((tm, LANES), lambda i: (i, 0))],
        out_shape=[jax.ShapeDtypeStruct((rows, D_MODEL), F32),
                   jax.ShapeDtypeStruct((rows, LANES), F32)],
        compiler_params=_cparams(("parallel",), VMEM_LIMIT),
        name=name,
    )(ya, yb, yc, ym, x, wo, g, b, rw, rb)


def _route(logits):
    top_v, top_i = lax.top_k(logits, TOP_K)
    gates = jax.nn.softmax(top_v, axis=-1)
    na = N_TOK * TOP_K
    flat_e = top_i.reshape(-1).astype(jnp.int32)
    order = jnp.argsort(flat_e).astype(jnp.int32)
    inv_order = jnp.argsort(order).astype(jnp.int32)
    se = flat_e[order]
    eids = jnp.arange(N_EXP, dtype=jnp.int32)
    counts = jnp.sum((flat_e[:, None] == eids[None, :]).astype(jnp.int32), axis=0)
    padded = (counts + MOE_TB - 1) // MOE_TB * MOE_TB
    pend = jnp.cumsum(padded)
    pstart = pend - padded
    ustart = jnp.cumsum(counts) - counts
    dest = pstart[se] + jnp.arange(na, dtype=jnp.int32) - ustart[se]
    pos = dest[inv_order]
    blk_start = jnp.arange(MOE_NB, dtype=jnp.int32) * MOE_TB
    blk_e = jnp.minimum(jnp.sum((pend[None, :] <= blk_start[:, None]).astype(jnp.int32), axis=1),
                        N_EXP - 1)
    off = (blk_start - pstart[blk_e])[:, None] + jnp.arange(MOE_TB, dtype=jnp.int32)[None, :]
    src = jnp.clip(ustart[blk_e][:, None] + off, 0, na - 1)
    row_tok = jnp.where(off < counts[blk_e][:, None], order[src] // TOP_K, 0).reshape(-1)
    n_used = (pend[-1] // MOE_TB).astype(jnp.int32).reshape(1)
    return gates, row_tok, pos, blk_e, n_used


def _row_copy(src_hbm, idx, dst, slot, sem):
    return pltpu.make_async_copy(src_hbm.at[pl.ds(idx, 1)], dst.at[pl.ds(slot, 1)], sem)


def _issue_rows(src_hbm, idx_ref, n, dst, sem):
    def body(r, carry):
        _row_copy(src_hbm, idx_ref[r], dst, r, sem).start()
        return carry

    lax.fori_loop(0, n, body, 0, unroll=8)


def _wait_rows(src_hbm, n, dst, sem):
    pltpu.make_async_copy(src_hbm.at[pl.ds(0, n)], dst, sem).wait()


def _gather_kernel(nused_ref, tok_ref, tok_next_ref, x_hbm, o_ref, buf, sem):
    i = pl.program_id(0)
    nu = nused_ref[0]
    slot = i % 2

    @pl.when(i == 0)
    def _():
        _issue_rows(x_hbm, tok_ref, MOE_TB, buf.at[0], sem.at[0])

    @pl.when(i + 1 < nu)
    def _():
        _issue_rows(x_hbm, tok_next_ref, MOE_TB, buf.at[1 - slot], sem.at[1 - slot])

    @pl.when(i < nu)
    def _():
        _wait_rows(x_hbm, MOE_TB, buf.at[slot], sem.at[slot])
        o_ref[...] = buf[slot].astype(BF16)

    @pl.when(i >= nu)
    def _():
        o_ref[...] = jnp.zeros_like(o_ref)


def _moe_gather(x1, row_tok, n_used):
    return pl.pallas_call(
        _gather_kernel,
        grid_spec=pltpu.PrefetchScalarGridSpec(
            num_scalar_prefetch=1,
            grid=(MOE_NB,),
            in_specs=[pl.BlockSpec((MOE_TB,), lambda i, nu: (i,), memory_space=pltpu.SMEM),
                      pl.BlockSpec((MOE_TB,), lambda i, nu: (jnp.minimum(i + 1, MOE_NB - 1),),
                                   memory_space=pltpu.SMEM),
                      pl.BlockSpec(memory_space=pl.ANY)],
            out_specs=pl.BlockSpec((MOE_TB, D_MODEL), lambda i, nu: (i, 0)),
            scratch_shapes=[pltpu.VMEM((2, MOE_TB, D_MODEL), F32), pltpu.SemaphoreType.DMA((2,))],
        ),
        out_shape=jax.ShapeDtypeStruct((MOE_P, D_MODEL), BF16),
        compiler_params=_cparams(("arbitrary",)),
        name="moe_gather",
    )(n_used, row_tok, row_tok, x1)


def _expert_up_kernel(be_ref, nused_ref, x_ref, wg_ref, wu_ref, bg_ref, bu_ref, h_ref):
    i = pl.program_id(1)

    @pl.when(i < nused_ref[0])
    def _():
        x = x_ref[...]
        gate = jnp.dot(x, wg_ref[0, 0], preferred_element_type=F32) + bg_ref[0, 0]
        up = jnp.dot(x, wu_ref[0, 0], preferred_element_type=F32) + bu_ref[0, 0]
        gate = jnp.minimum(gate, SWIGLU_LIMIT)
        up = jnp.clip(up, -SWIGLU_LIMIT, SWIGLU_LIMIT)
        h_ref[...] = ((up + 1.0) * (gate * jax.nn.sigmoid(SWIGLU_ALPHA * gate))).astype(h_ref.dtype)

    @pl.when(i >= nused_ref[0])
    def _():
        h_ref[...] = jnp.zeros_like(h_ref)


def _expert_up(xg, w_gu, b_gu, blk_e, n_used, layer):
    nf = D_FF // MOE_TF

    def used(i, nu):
        return jnp.minimum(i, nu[0] - 1)

    def wspec(rows, col0):
        return pl.BlockSpec((1, 1, rows, MOE_TF),
                            lambda j, i, be, nu: (layer, be[used(i, nu)], 0, col0 + j))

    return pl.pallas_call(
        _expert_up_kernel,
        grid_spec=pltpu.PrefetchScalarGridSpec(
            num_scalar_prefetch=2,
            grid=(nf, MOE_NB),
            in_specs=[
                pl.BlockSpec((MOE_TB, D_MODEL), lambda j, i, be, nu: (used(i, nu), 0)),
                wspec(D_MODEL, 0), wspec(D_MODEL, nf), wspec(1, 0), wspec(1, nf),
            ],
            out_specs=pl.BlockSpec((MOE_TB, MOE_TF), lambda j, i, be, nu: (i, j)),
        ),
        out_shape=jax.ShapeDtypeStruct((MOE_P, D_FF), BF16),
        compiler_params=_cparams(("arbitrary", "arbitrary"), VMEM_LIMIT),
        name="expert_up",
    )(blk_e, n_used, xg, w_gu, w_gu, b_gu, b_gu)


def _expert_down_kernel(be_ref, nused_ref, h_ref, w_ref, b_ref, y_ref):
    i = pl.program_id(0)

    @pl.when(i < nused_ref[0])
    def _():
        y_ref[...] = jnp.dot(h_ref[...], w_ref[0, 0], preferred_element_type=F32) + b_ref[0, 0]

    @pl.when(i >= nused_ref[0])
    def _():
        y_ref[...] = jnp.zeros_like(y_ref)


def _expert_down(h, w_dn, b_dn, blk_e, n_used, layer):
    def used(i, nu):
        return jnp.minimum(i, nu[0] - 1)

    return pl.pallas_call(
        _expert_down_kernel,
        grid_spec=pltpu.PrefetchScalarGridSpec(
            num_scalar_prefetch=2,
            grid=(MOE_NB,),
            in_specs=[
                pl.BlockSpec((MOE_TB, D_FF), lambda i, be, nu: (used(i, nu), 0)),
                pl.BlockSpec((1, 1, D_FF, D_MODEL),
                             lambda i, be, nu: (layer, be[used(i, nu)], 0, 0)),
                pl.BlockSpec((1, 1, 1, D_MODEL),
                             lambda i, be, nu: (layer, be[used(i, nu)], 0, 0)),
            ],
            out_specs=pl.BlockSpec((MOE_TB, D_MODEL), lambda i, be, nu: (i, 0)),
        ),
        out_shape=jax.ShapeDtypeStruct((MOE_P, D_MODEL), F32),
        compiler_params=_cparams(("arbitrary",), VMEM_LIMIT),
        name="expert_down",
    )(blk_e, n_used, h, w_dn, b_dn)


def _combine_kernel(pos_ref, pos_next_ref, gates_ref, x1_ref, y_hbm, g_ref, b_ref, x2_ref, x2b_ref,
                    buf, sem):
    i = pl.program_id(0)
    slot = i % 2
    n = CMB_T * TOP_K

    @pl.when(i == 0)
    def _():
        _issue_rows(y_hbm, pos_ref, n, buf.at[0], sem.at[0])

    @pl.when(i + 1 < pl.num_programs(0))
    def _():
        _issue_rows(y_hbm, pos_next_ref, n, buf.at[1 - slot], sem.at[1 - slot])

    _wait_rows(y_hbm, n, buf.at[slot], sem.at[slot])
    gates = gates_ref[...]
    y = jnp.zeros((CMB_T, D_MODEL), F32)
    for k in range(TOP_K):
        y = y + buf[slot, k * CMB_T:(k + 1) * CMB_T, :] * gates[:, k:k + 1]
    x2 = _layernorm(DN_ALPHA * x1_ref[...] + y, g_ref[...], b_ref[...])
    x2_ref[...] = x2
    x2b_ref[...] = x2.astype(BF16)


def _moe_combine(pos_kmajor, gates, x1, y, g, b):
    row = pl.BlockSpec((CMB_T, D_MODEL), lambda i: (i, 0))
    vec = pl.BlockSpec((1, D_MODEL), lambda i: (0, 0))
    steps = N_TOK // CMB_T
    return pl.pallas_call(
        _combine_kernel,
        grid=(steps,),
        in_specs=[pl.BlockSpec((CMB_T * TOP_K,), lambda i: (i,), memory_space=pltpu.SMEM),
                  pl.BlockSpec((CMB_T * TOP_K,), lambda i: (jnp.minimum(i + 1, steps - 1),),
                               memory_space=pltpu.SMEM),
                  pl.BlockSpec((CMB_T, TOP_K), lambda i: (i, 0)),
                  row, pl.BlockSpec(memory_space=pl.ANY), vec, vec],
        out_specs=[row, row],
        out_shape=[jax.ShapeDtypeStruct((N_TOK, D_MODEL), F32),
                   jax.ShapeDtypeStruct((N_TOK, D_MODEL), BF16)],
        scratch_shapes=[pltpu.VMEM((2, CMB_T * TOP_K, D_MODEL), F32),
                        pltpu.SemaphoreType.DMA((2,))],
        compiler_params=_cparams(("arbitrary",)),
        name="moe_combine",
    )(pos_kmajor, pos_kmajor, gates, x1, y, g, b)


def _permute_w_in(w):
    parts, off = [], 0
    for n in REF_IN_SIZES:
        parts.append((off, n))
        off += n
    return jnp.concatenate([w[:, parts[i][0]:parts[i][0] + parts[i][1]] for i in MY_ORDER],
                           axis=1).astype(BF16)


def kernel(x_prompt, x_sample, cache_conv, state_ret, cache_swa_k, cache_swa_v, cache_mem_k, cache_mem_v, mem_prompt, w_in, conv_w, ret_gn_w, attn_sinks, w_mem_kv, w_o, ln1_g, ln1_b, router_w, router_b, w_gate_up, b_gate_up, w_down, b_down, ln2_g, ln2_b):
    pos_p = jnp.arange(SEQ, dtype=jnp.int32)
    pos_s = PAST_LEN + jnp.arange(DEC_SEQ, dtype=jnp.int32)
    ret_tabs_p = _ret_tables(pos_p, R_CHUNK)
    ret_tabs_s = _ret_tables(pos_s, DEC_SEQ)
    swa_tabs_p = _swa_tables(pos_p)
    swa_tabs_s = _swa_tables(pos_s)
    kvw = A_KV * A_HD

    xp = x_prompt.reshape(N_PROMPT, D_MODEL)
    xs = x_sample.reshape(N_SAMPLE, D_MODEL)
    xb = jnp.concatenate([xp, xs], axis=0).astype(BF16)
    mem_b = mem_prompt.reshape(BATCH * N_MEM, D_MODEL).astype(BF16)
    zeros_e = jnp.zeros((256, D_MODEL), F32)
    w_gu_b = w_gate_up.astype(BF16)
    w_dn_b = w_down.astype(BF16)
    b_gu = b_gate_up.reshape(DEPTH, N_EXP, 1, 2 * D_FF)
    b_dn = b_down.reshape(DEPTH, N_EXP, 1, D_MODEL)
    swa_k = cache_swa_k.reshape(DEPTH, DEC_BATCH, WINDOW, kvw)
    swa_v = cache_swa_v.reshape(DEPTH, DEC_BATCH, WINDOW, kvw)

    outs = {k: [] for k in ("p_conv", "p_ret", "p_k", "p_v", "p_mk", "p_mv",
                            "s_conv", "s_ret", "s_k", "s_v")}
    for l in range(DEPTH):
        z = _matmul(xb, _permute_w_in(w_in[l]), 1536, 512)
        gn = ret_gn_w[l].reshape(1, D_MODEL)
        sinks = attn_sinks[l]

        ya_p, tail_p = _conv_branch(z, conv_w[l], zeros_e, zeros_e, 0, N_PROMPT, SEQ, False)
        yb_p, ret_p = _ret_prompt(z, ret_tabs_p, gn)
        yc_p, kr_p = _swa_prompt(z, sinks, swa_tabs_p)
        mkv = _matmul(mem_b, w_mem_kv[l].astype(BF16), 512, 512)
        mk_p = mkv[:, :D_MODEL].reshape(BATCH, N_MEM, D_MODEL)
        mv_p = mkv[:, D_MODEL:].reshape(BATCH, N_MEM, D_MODEL)
        ym_p = _mem_attend(z, mk_p, mv_p, 0, BATCH, SEQ, 512, "mem_prompt")

        ea = jnp.repeat(cache_conv[l][:, 0], DEC_SEQ, axis=0)
        eb = jnp.repeat(cache_conv[l][:, 1], DEC_SEQ, axis=0)
        ya_s, u_s = _conv_branch(z, conv_w[l], ea, eb, N_PROMPT, N_SAMPLE, DEC_SEQ, True)
        yb_s, ret_s = _ret_sample(z, ret_tabs_s, gn, state_ret, l)
        yc_s, nk_s, nv_s = _swa_sample(z, sinks, swa_tabs_s, swa_k, swa_v, l)
        ym_s = _mem_attend(z, cache_mem_k, cache_mem_v, N_PROMPT, DEC_BATCH, DEC_SEQ, DEC_SEQ,
                           "mem_sample", layer=l)

        wo_b = w_o[l].astype(BF16)
        g1 = ln1_g[l].reshape(1, D_MODEL)
        b1 = ln1_b[l].reshape(1, D_MODEL)
        rw = jnp.pad(router_w[l], ((0, 0), (0, LANES - N_EXP)))
        rb = jnp.pad(router_b[l], (0, LANES - N_EXP)).reshape(1, LANES)
        x1_p, lg_p = _wo_ln(ya_p, yb_p, yc_p, ym_p, xp, wo_b, g1, b1, rw, rb, "wo_ln_prompt")
        x1_s, lg_s = _wo_ln(ya_s, yb_s, yc_s, ym_s, xs, wo_b, g1, b1, rw, rb, "wo_ln_sample")
        x1 = jnp.concatenate([x1_p, x1_s], axis=0)
        logits = jnp.concatenate([lg_p, lg_s], axis=0)[:, :N_EXP]

        gates, row_tok, pos, blk_e, n_used = _route(logits)
        xg = _moe_gather(x1, row_tok, n_used)
        h = _expert_up(xg, w_gu_b, b_gu, blk_e, n_used, l)
        y = _expert_down(h, w_dn_b, b_dn, blk_e, n_used, l)
        pos_k = pos.reshape(N_TOK // CMB_T, CMB_T, TOP_K).transpose(0, 2, 1).reshape(-1)
        x2, xb = _moe_combine(pos_k, gates, x1, y, ln2_g[l].reshape(1, D_MODEL),
                              ln2_b[l].reshape(1, D_MODEL))
        xp, xs = x2[:N_PROMPT], x2[N_PROMPT:]

        tail_p = tail_p.reshape(BATCH, SEQ // 256, SUBLANES, D_MODEL)
        outs["p_conv"].append(tail_p[:, -1, -(CONV_W - 1):])
        outs["p_ret"].append(ret_p)
        outs["p_k"].append(kr_p.reshape(BATCH, SEQ, A_KV, A_HD)[:, -WINDOW:])
        outs["p_v"].append(z[:N_PROMPT, AV:AV + kvw].reshape(BATCH, SEQ, A_KV, A_HD)[:, -WINDOW:])
        outs["p_mk"].append(mk_p.reshape(BATCH, N_MEM, M_HEADS, M_HD))
        outs["p_mv"].append(mv_p.reshape(BATCH, N_MEM, M_HEADS, M_HD))
        outs["s_conv"].append(u_s.reshape(DEC_BATCH, DEC_SEQ, D_MODEL)[:, -(CONV_W - 1):])
        outs["s_ret"].append(ret_s)
        outs["s_k"].append(nk_s.reshape(DEC_BATCH, WINDOW, A_KV, A_HD))
        outs["s_v"].append(nv_s.reshape(DEC_BATCH, WINDOW, A_KV, A_HD))

    return (xp.reshape(BATCH, SEQ, D_MODEL), xs.reshape(DEC_BATCH, DEC_SEQ, D_MODEL),
            jnp.stack(outs["p_conv"]), jnp.stack(outs["p_ret"]), jnp.stack(outs["p_k"]),
            jnp.stack(outs["p_v"]), jnp.stack(outs["p_mk"]), jnp.stack(outs["p_mv"]),
            jnp.stack(outs["s_conv"]), jnp.stack(outs["s_ret"]), jnp.stack(outs["s_k"]),
            jnp.stack(outs["s_v"]))
```

```python
import functools

import jax
import jax.numpy as jnp
from jax import lax
from jax.experimental import pallas as pl
from jax.experimental.pallas import tpu as pltpu

F32 = jnp.float32
BF16 = jnp.bfloat16

D_MODEL = 2048
BATCH = 2
SEQ = 4096
DEPTH = 2
DEC_BATCH = 128
DEC_SEQ = 8
PAST_LEN = 8192
N_PROMPT = BATCH * SEQ
N_SAMPLE = DEC_BATCH * DEC_SEQ
N_TOK = N_PROMPT + N_SAMPLE

CONV_W = 3
R_HEADS = 8
R_DK = 128
R_DV = D_MODEL // R_HEADS
R_CHUNK = 128
R_THETA = 10000.0
A_HEADS = 32
A_KV = 4
A_HD = D_MODEL // A_HEADS
A_GROUP = A_HEADS // A_KV
WINDOW = 128
ROPE_DIMS = A_HD // 4
ROPE_THETA = 500000.0
N_MEM = 256
M_HEADS = 4
M_HD = D_MODEL // M_HEADS
N_EXP = 32
TOP_K = 4
D_FF = D_MODEL
SWIGLU_LIMIT = 7.0
SWIGLU_ALPHA = 1.702
LN_EPS = 1e-5
GN_EPS = 1e-6
DN_ALPHA = (2 * DEPTH) ** 0.25

REF_IN_SIZES = (D_MODEL, D_MODEL, D_MODEL, R_HEADS * R_DK, R_HEADS * R_DK, D_MODEL, D_MODEL,
                D_MODEL, A_KV * A_HD, A_KV * A_HD, D_MODEL, 4 * D_MODEL)
D_IN = sum(REF_IN_SIZES)
MY_ORDER = (11, 0, 1, 2, 5, 6, 7, 10, 3, 4, 8, 9)
GL, HC, BC, CC, RV, RG, AQ, MQ, RQ, RK, AK, AV = (
    0, 8192, 10240, 12288, 14336, 16384, 18432, 20480, 22528, 23552, 24576, 24832)

LANES = 128
SUBLANES = 8
VMEM_LIMIT = 56 * 1024 * 1024

MOE_TB = 256
MOE_NB = N_TOK * TOP_K // MOE_TB + N_EXP
MOE_P = MOE_NB * MOE_TB
MOE_TF = 512
MOE_TN = 1024
CMB_T = 64


def _cparams(sem, vmem=None):
    return pltpu.CompilerParams(dimension_semantics=sem, vmem_limit_bytes=vmem)


def _mm_kernel(x_ref, w_ref, o_ref):
    o_ref[...] = jnp.dot(x_ref[...], w_ref[...], preferred_element_type=F32).astype(o_ref.dtype)


def _matmul(x, w, tm, tn, out_dtype=F32):
    m, k = x.shape
    n = w.shape[1]
    return pl.pallas_call(
        _mm_kernel,
        grid=(m // tm, n // tn),
        in_specs=[pl.BlockSpec((tm, k), lambda i, j: (i, 0)),
                  pl.BlockSpec((k, tn), lambda i, j: (0, j))],
        out_specs=pl.BlockSpec((tm, tn), lambda i, j: (i, j)),
        out_shape=jax.ShapeDtypeStruct((m, n), out_dtype),
        compiler_params=_cparams(("parallel", "parallel"), VMEM_LIMIT),
        name="matmul",
    )(x, w)


def _conv_kernel(seq_len, full_u, hc_ref, cc_ref, bc_ref, g_ref, hh_ref, ch_ref, ea_ref, eb_ref,
                 w_ref, ya_ref, tail_ref):
    i = pl.program_id(0)
    tm = hc_ref.shape[0]
    u = cc_ref[...] * hc_ref[...]
    ue = jnp.concatenate([ch_ref[...] * hh_ref[...], u], axis=0)
    p1 = pltpu.roll(ue, 1, 0)[SUBLANES:]
    p2 = pltpu.roll(ue, 2, 0)[SUBLANES:]
    t = (i * tm + lax.broadcasted_iota(jnp.int32, (tm, 1), 0)) % seq_len
    ea = ea_ref[...]
    eb = eb_ref[...]
    p1 = jnp.where(t == 0, eb, p1)
    p2 = jnp.where(t == 0, ea, jnp.where(t == 1, eb, p2))
    w = w_ref[...]
    y = p2 * w[0:1] + p1 * w[1:2] + u * w[2:3]
    ya_ref[...] = jax.nn.sigmoid(g_ref[...]) * (bc_ref[...] * y)
    tail_ref[...] = u if full_u else u[tm - SUBLANES:]


def _conv_branch(z, conv_w, ea, eb, row0, rows, seq_len, sample):
    tm, tc = 256, 512
    r0 = row0 // tm
    nj = D_MODEL // tc

    def zspec(col):
        return pl.BlockSpec((tm, tc), lambda i, j: (r0 + i, col // tc + j))

    def halo(col):
        return pl.BlockSpec(
            (SUBLANES, tc),
            lambda i, j: (jnp.maximum((r0 + i) * (tm // SUBLANES) - 1, 0), col // tc + j))

    if sample:
        espec = pl.BlockSpec((tm, tc), lambda i, j: (i, j))
        tail_spec = pl.BlockSpec((tm, tc), lambda i, j: (i, j))
        tail_shape = jax.ShapeDtypeStruct((rows, D_MODEL), F32)
    else:
        espec = pl.BlockSpec((tm, tc), lambda i, j: (0, j))
        tail_spec = pl.BlockSpec((SUBLANES, tc), lambda i, j: (i, j))
        tail_shape = jax.ShapeDtypeStruct((rows // tm * SUBLANES, D_MODEL), F32)
    return pl.pallas_call(
        functools.partial(_conv_kernel, seq_len, sample),
        grid=(rows // tm, nj),
        in_specs=[zspec(HC), zspec(CC), zspec(BC), zspec(GL), halo(HC), halo(CC), espec, espec,
                  pl.BlockSpec((CONV_W, tc), lambda i, j: (0, j))],
        out_specs=[pl.BlockSpec((tm, tc), lambda i, j: (i, j)), tail_spec],
        out_shape=[jax.ShapeDtypeStruct((rows, D_MODEL), F32), tail_shape],
        compiler_params=_cparams(("parallel", "parallel"), VMEM_LIMIT),
        name="conv_sample" if sample else "conv_prompt",
    )(z, z, z, z, z, z, ea, eb, conv_w)


def _rope_full(x, cos, sin):
    return x * cos + pltpu.roll(x, R_DK // 2, 1) * sin


def _bf16_exact(x):
    return x.astype(BF16).astype(F32)


def _group_norm_gate(o, gn, rg, g):
    mu = jnp.mean(o, axis=-1, keepdims=True)
    d = o - mu
    var = jnp.mean(d * d, axis=-1, keepdims=True)
    yn = d * lax.rsqrt(var + GN_EPS) * gn
    return jax.nn.sigmoid(g) * ((rg * jax.nn.sigmoid(rg)) * yn)


def _ret_prompt_kernel(q_ref, k_ref, v_ref, rg_ref, g_ref, cos_ref, sin_ref, dmask_ref, qdec_ref,
                       kdec_ref, cdec_ref, gn_ref, y_ref, sfin_ref, s_scr):
    c = pl.program_id(1)

    @pl.when(c == 0)
    def _():
        s_scr[...] = jnp.zeros_like(s_scr)

    cos = cos_ref[...]
    sin = sin_ref[...]
    for h in range(R_HEADS):
        qs = slice(h * R_DK, (h + 1) * R_DK)
        vs = slice(h * R_DV, (h + 1) * R_DV)
        q = _rope_full(q_ref[:, qs], cos, sin)
        k = _rope_full(k_ref[:, qs], cos, sin) * (R_DK ** -0.5)
        vb = v_ref[:, vs].astype(BF16)
        s = s_scr[h]
        att = lax.dot_general(q.astype(BF16), k.astype(BF16), (((1,), (1,)), ((), ())),
                              preferred_element_type=F32) * dmask_ref[h]
        o = (jnp.dot(att.astype(BF16), vb, preferred_element_type=F32)
             + jnp.dot((q * qdec_ref[h]).astype(BF16), s.astype(BF16), preferred_element_type=F32))
        kd = (k * kdec_ref[h]).astype(BF16)
        s_new = cdec_ref[h] * s + lax.dot_general(kd, vb, (((0,), (0,)), ((), ())),
                                                  preferred_element_type=F32)
        s_scr[h] = s_new
        sfin_ref[0, h] = s_new
        y_ref[:, vs] = _group_norm_gate(o, gn_ref[:, vs], rg_ref[:, vs], g_ref[:, vs])


def _ret_prompt(z, tabs, gn_w):
    cos, sin, dmask, qdec, kdec, cdec = tabs
    nc = SEQ // R_CHUNK

    def zspec(col, width):
        return pl.BlockSpec((R_CHUNK, width), lambda b, c: (b * nc + c, col // width))

    def full(a):
        return pl.BlockSpec(a.shape, lambda b, c: (0,) * a.ndim)

    tspec = pl.BlockSpec((R_CHUNK, R_DK), lambda b, c: (c, 0))
    return pl.pallas_call(
        _ret_prompt_kernel,
        grid=(BATCH, nc),
        in_specs=[zspec(RQ, R_HEADS * R_DK), zspec(RK, R_HEADS * R_DK), zspec(RV, D_MODEL),
                  zspec(RG, D_MODEL), zspec(GL + D_MODEL, D_MODEL), tspec, tspec,
                  full(dmask), full(qdec), full(kdec), full(cdec), full(gn_w)],
        out_specs=[pl.BlockSpec((R_CHUNK, D_MODEL), lambda b, c: (b * nc + c, 0)),
                   pl.BlockSpec((1, R_HEADS, R_DK, R_DV), lambda b, c: (b, 0, 0, 0))],
        out_shape=[jax.ShapeDtypeStruct((N_PROMPT, D_MODEL), F32),
                   jax.ShapeDtypeStruct((BATCH, R_HEADS, R_DK, R_DV), F32)],
        scratch_shapes=[pltpu.VMEM((R_HEADS, R_DK, R_DV), F32)],
        compiler_params=_cparams(("parallel", "arbitrary"), VMEM_LIMIT),
        name="retention_prompt",
    )(z, z, z, z, z, cos, sin, dmask, qdec, kdec, cdec, gn_w)


def _ret_sample_kernel(q_ref, k_ref, v_ref, rg_ref, g_ref, cos_ref, sin_ref, dmask_ref, qdec_ref,
                       kdec_ref, cdec_ref, gn_ref, s_ref, y_ref, snew_ref):
    cos = cos_ref[...]
    sin = sin_ref[...]
    for h in range(R_HEADS):
        qs = slice(h * R_DK, (h + 1) * R_DK)
        vs = slice(h * R_DV, (h + 1) * R_DV)
        q = _rope_full(q_ref[:, qs], cos, sin)
        k = _rope_full(k_ref[:, qs], cos, sin) * (R_DK ** -0.5)
        v = _bf16_exact(v_ref[:, vs])
        s = s_ref[0, 0, h]
        att = lax.dot_general(_bf16_exact(q), _bf16_exact(k), (((1,), (1,)), ((), ())),
                              preferred_element_type=F32) * dmask_ref[h]
        o = (jnp.dot(_bf16_exact(att), v, preferred_element_type=F32)
             + jnp.dot((q * qdec_ref[h]).astype(BF16), s.astype(BF16), preferred_element_type=F32))
        kd = _bf16_exact(k * kdec_ref[h])
        snew_ref[0, h] = cdec_ref[h] * s + lax.dot_general(
            kd, v, (((0,), (0,)), ((), ())), preferred_element_type=F32)
        y_ref[:, vs] = _group_norm_gate(o, gn_ref[:, vs], rg_ref[:, vs], g_ref[:, vs])


def _ret_sample(z, tabs, gn_w, state, layer):
    cos, sin, dmask, qdec, kdec, cdec = tabs
    r0 = N_PROMPT // DEC_SEQ

    def zspec(col, width):
        return pl.BlockSpec((DEC_SEQ, width), lambda b: (r0 + b, col // width))

    def full(a):
        return pl.BlockSpec(a.shape, lambda b: (0,) * a.ndim)

    sspec = pl.BlockSpec((1, R_HEADS, R_DK, R_DV), lambda b: (b, 0, 0, 0))
    sin_spec = pl.BlockSpec((1, 1, R_HEADS, R_DK, R_DV), lambda b: (layer, b, 0, 0, 0))
    return pl.pallas_call(
        _ret_sample_kernel,
        grid=(DEC_BATCH,),
        in_specs=[zspec(RQ, R_HEADS * R_DK), zspec(RK, R_HEADS * R_DK), zspec(RV, D_MODEL),
                  zspec(RG, D_MODEL), zspec(GL + D_MODEL, D_MODEL),
                  full(cos), full(sin), full(dmask), full(qdec), full(kdec), full(cdec), full(gn_w),
                  sin_spec],
        out_specs=[pl.BlockSpec((DEC_SEQ, D_MODEL), lambda b: (b, 0)), sspec],
        out_shape=[jax.ShapeDtypeStruct((N_SAMPLE, D_MODEL), F32),
                   jax.ShapeDtypeStruct(state.shape[1:], F32)],
        compiler_params=_cparams(("parallel",)),
        name="retention_sample",
    )(z, z, z, z, z, cos, sin, dmask, qdec, kdec, cdec, gn_w, state)


def _ret_tables(pos, chunk):
    half = R_DK // 2
    inv = R_THETA ** (-jnp.arange(half, dtype=F32) / half)
    ang = pos.astype(F32)[:, None] * inv[None, :]
    cos = jnp.concatenate([jnp.cos(ang), jnp.cos(ang)], axis=-1)
    sin = jnp.concatenate([-jnp.sin(ang), jnp.sin(ang)], axis=-1)
    lg = jnp.log(1.0 - 2.0 ** (-5.0 - jnp.arange(R_HEADS, dtype=F32)))
    i = jnp.arange(chunk, dtype=F32)
    diff = i[:, None] - i[None, :]
    dmask = jnp.where(diff >= 0, jnp.exp(lg[:, None, None] * jnp.maximum(diff, 0.0)), 0.0)
    qdec = jnp.broadcast_to(jnp.exp(lg[:, None] * (i + 1.0))[:, :, None], (R_HEADS, chunk, R_DK))
    kdec = jnp.broadcast_to(jnp.exp(lg[:, None] * (chunk - 1.0 - i))[:, :, None],
                            (R_HEADS, chunk, R_DK))
    cdec = jnp.broadcast_to(jnp.exp(lg * chunk)[:, None, None], (R_HEADS, 1, R_DV))
    return cos, sin, dmask.astype(F32), qdec.astype(F32), kdec.astype(F32), cdec.astype(F32)


def _swa_tables(pos):
    half = ROPE_DIMS // 2
    inv = ROPE_THETA ** (-jnp.arange(half, dtype=F32) / half)
    ang = pos.astype(F32)[:, None] * inv[None, :]
    cos, sin = jnp.cos(ang), jnp.sin(ang)
    t = pos.shape[0]
    ones = jnp.ones((t, A_HD - ROPE_DIMS), F32)
    zeros = jnp.zeros((t, A_HD - ROPE_DIMS), F32)
    zh = jnp.zeros((t, half), F32)
    c = jnp.concatenate([cos, cos, ones], axis=-1)
    s_lo = jnp.concatenate([-sin, zh, zeros], axis=-1)
    s_hi = jnp.concatenate([zh, sin, zeros], axis=-1)
    rep = LANES // A_HD
    return jnp.tile(c, (1, rep)), jnp.tile(s_lo, (1, rep)), jnp.tile(s_hi, (1, rep))


def _rope_partial(x, c, s_lo, s_hi):
    half = ROPE_DIMS // 2
    return x * c + pltpu.roll(x, LANES - half, 1) * s_lo + pltpu.roll(x, half, 1) * s_hi


def _dup_head(chunk, odd):
    lane = lax.broadcasted_iota(jnp.int32, chunk.shape, 1)
    sw = pltpu.roll(chunk, A_HD, 1)
    if odd:
        return jnp.where(lane < A_HD, sw, chunk)
    return jnp.where(lane < A_HD, chunk, sw)


def _swa_core(sink_ref, q_ref, g_ref, y_ref, kp, kc, vp, vc, tabs_q, has_prev, tq, chunks_per_dot):
    cq, slq, shq = tabs_q
    lane_q = lax.broadcasted_iota(jnp.int32, (tq, LANES), 1)
    n_chunks = A_GROUP * A_HD // LANES
    for h in range(A_KV):
        kchunk = slice((h // 2) * LANES, (h // 2 + 1) * LANES)
        kkp = _dup_head(kp[:, kchunk], h % 2).astype(BF16)
        kkc = _dup_head(kc[:, kchunk], h % 2).astype(BF16)
        vvp = _dup_head(vp[:, kchunk], h % 2).astype(BF16)
        vvc = _dup_head(vc[:, kchunk], h % 2).astype(BF16)
        for c0 in range(0, n_chunks, chunks_per_dot):
            pieces, sinks = [], []
            for ci in range(c0, c0 + chunks_per_dot):
                col = h * A_GROUP * A_HD + ci * LANES
                qc = _rope_partial(q_ref[:, col:col + LANES], cq, slq, shq) * (A_HD ** -0.5)
                for half in range(2):
                    keep = (lane_q < A_HD) if half == 0 else (lane_q >= A_HD)
                    pieces.append(jnp.where(keep, qc, 0.0).astype(BF16))
                    head = h * A_GROUP + 2 * ci + half
                    sinks.append(jnp.full((tq, 1), sink_ref[head], F32))
            qs = jnp.concatenate(pieces, axis=0)
            sink = jnp.concatenate(sinks, axis=0)
            r = qs.shape[0]
            nt = (((1,), (1,)), ((), ()))
            sp = lax.dot_general(qs, kkp, nt, preferred_element_type=F32)
            sc = lax.dot_general(qs, kkc, nt, preferred_element_type=F32)
            t = lax.broadcasted_iota(jnp.int32, (r, WINDOW), 0) % tq
            j = lax.broadcasted_iota(jnp.int32, (r, WINDOW), 1)
            sp = jnp.where((j >= t) & has_prev, sp, -jnp.inf)
            sc = jnp.where(j <= t, sc, -jnp.inf)
            m = jnp.maximum(jnp.max(jnp.maximum(sp, sc), axis=-1, keepdims=True), sink)
            pp = jnp.exp(sp - m)
            pc = jnp.exp(sc - m)
            den = jnp.sum(pp + pc, axis=-1, keepdims=True) + jnp.exp(sink - m)
            inv = 1.0 / den
            o = (jnp.dot((pp * inv).astype(BF16), vvp, preferred_element_type=F32)
                 + jnp.dot((pc * inv).astype(BF16), vvc, preferred_element_type=F32))
            for n, ci in enumerate(range(c0, c0 + chunks_per_dot)):
                col = h * A_GROUP * A_HD + ci * LANES
                oa = o[(2 * n) * tq:(2 * n + 1) * tq]
                ob = o[(2 * n + 1) * tq:(2 * n + 2) * tq]
                oc = jnp.where(lane_q < A_HD, oa, ob)
                y_ref[:, col:col + LANES] = jax.nn.sigmoid(g_ref[:, col:col + LANES]) * oc


def _rope_kv(k_ref, ck, slk, shk):
    parts = [_rope_partial(k_ref[:, c * LANES:(c + 1) * LANES], ck, slk, shk)
             for c in range(A_KV * A_HD // LANES)]
    return jnp.concatenate(parts, axis=-1)


def _swa_prompt_kernel(sink_ref, q_ref, k_ref, v_ref, g_ref, c_ref, sl_ref, sh_ref,
                       y_ref, kr_ref, kp_scr, vp_scr):
    n = pl.program_id(1)

    @pl.when(n == 0)
    def _():
        kp_scr[...] = jnp.zeros_like(kp_scr)
        vp_scr[...] = jnp.zeros_like(vp_scr)

    tabs = (c_ref[...], sl_ref[...], sh_ref[...])
    kc = _rope_kv(k_ref, *tabs)
    vc = v_ref[...]
    kr_ref[...] = kc
    _swa_core(sink_ref, q_ref, g_ref, y_ref, kp_scr[...], kc, vp_scr[...], vc, tabs, n > 0,
              WINDOW, 1)
    kp_scr[...] = kc
    vp_scr[...] = vc


def _swa_prompt(z, sinks, tabs):
    nb = SEQ // WINDOW
    kvw = A_KV * A_HD

    def zspec(col, width):
        return pl.BlockSpec((WINDOW, width), lambda b, n: (b * nb + n, col // width))

    tspec = pl.BlockSpec((WINDOW, LANES), lambda b, n: (n, 0))
    return pl.pallas_call(
        _swa_prompt_kernel,
        grid=(BATCH, nb),
        in_specs=[pl.BlockSpec(memory_space=pltpu.SMEM),
                  zspec(AQ, D_MODEL), zspec(AK, kvw), zspec(AV, kvw), zspec(GL + 2 * D_MODEL, D_MODEL),
                  tspec, tspec, tspec],
        out_specs=[pl.BlockSpec((WINDOW, D_MODEL), lambda b, n: (b * nb + n, 0)),
                   pl.BlockSpec((WINDOW, kvw), lambda b, n: (b * nb + n, 0))],
        out_shape=[jax.ShapeDtypeStruct((N_PROMPT, D_MODEL), F32),
                   jax.ShapeDtypeStruct((N_PROMPT, kvw), F32)],
        scratch_shapes=[pltpu.VMEM((WINDOW, kvw), F32), pltpu.VMEM((WINDOW, kvw), F32)],
        compiler_params=_cparams(("parallel", "arbitrary"), VMEM_LIMIT),
        name="swa_prompt",
    )(sinks, z, z, z, z, *tabs)


def _swa_sample_kernel(sink_ref, q_ref, k_ref, v_ref, g_ref, c_ref, sl_ref, sh_ref, ck_ref, cv_ref,
                       y_ref, nk_ref, nv_ref):
    tabs = (c_ref[...], sl_ref[...], sh_ref[...])
    kn = _rope_kv(k_ref, *tabs)
    vn = v_ref[...]
    pad = jnp.zeros((WINDOW - DEC_SEQ, A_KV * A_HD), F32)
    kp = ck_ref[0, 0]
    vp = cv_ref[0, 0]
    _swa_core(sink_ref, q_ref, g_ref, y_ref, kp, jnp.concatenate([kn, pad], axis=0),
              vp, jnp.concatenate([vn, pad], axis=0), tabs, True, DEC_SEQ,
              A_GROUP * A_HD // LANES)
    nk_ref[0] = jnp.concatenate([kp[DEC_SEQ:], kn], axis=0)
    nv_ref[0] = jnp.concatenate([vp[DEC_SEQ:], vn], axis=0)


def _swa_sample(z, sinks, tabs, cache_k, cache_v, layer):
    kvw = A_KV * A_HD
    r0 = N_PROMPT // DEC_SEQ

    def zspec(col, width):
        return pl.BlockSpec((DEC_SEQ, width), lambda b: (r0 + b, col // width))

    tspec = pl.BlockSpec((DEC_SEQ, LANES), lambda b: (0, 0))
    cspec = pl.BlockSpec((1, WINDOW, kvw), lambda b: (b, 0, 0))
    cin_spec = pl.BlockSpec((1, 1, WINDOW, kvw), lambda b: (layer, b, 0, 0))
    return pl.pallas_call(
        _swa_sample_kernel,
        grid=(DEC_BATCH,),
        in_specs=[pl.BlockSpec(memory_space=pltpu.SMEM),
                  zspec(AQ, D_MODEL), zspec(AK, kvw), zspec(AV, kvw), zspec(GL + 2 * D_MODEL, D_MODEL),
                  tspec, tspec, tspec, cin_spec, cin_spec],
        out_specs=[pl.BlockSpec((DEC_SEQ, D_MODEL), lambda b: (b, 0)), cspec, cspec],
        out_shape=[jax.ShapeDtypeStruct((N_SAMPLE, D_MODEL), F32),
                   jax.ShapeDtypeStruct((DEC_BATCH, WINDOW, kvw), F32),
                   jax.ShapeDtypeStruct((DEC_BATCH, WINDOW, kvw), F32)],
        compiler_params=_cparams(("parallel",)),
        name="swa_sample",
    )(sinks, z, z, z, z, *tabs, cache_k, cache_v)


def _mem_kernel(q_ref, k_ref, v_ref, g_ref, y_ref):
    for h in range(M_HEADS):
        hs = slice(h * M_HD, (h + 1) * M_HD)
        q = q_ref[:, hs].astype(BF16)
        k = k_ref[0, :, hs].astype(BF16)
        v = v_ref[0, :, hs].astype(BF16)
        s = lax.dot_general(q, k, (((1,), (1,)), ((), ())), preferred_element_type=F32) * (M_HD ** -0.5)
        m = jnp.max(s, axis=-1, keepdims=True)
        p = jnp.exp(s - m)
        p = p / jnp.sum(p, axis=-1, keepdims=True)
        o = jnp.dot(p.astype(BF16), v, preferred_element_type=F32)
        y_ref[:, hs] = jax.nn.sigmoid(g_ref[:, hs]) * o


def _mem_cache_kernel(q_ref, k_ref, v_ref, g_ref, y_ref):
    k2 = k_ref.reshape(N_MEM * M_HEADS, M_HD)
    v2 = v_ref.reshape(N_MEM * M_HEADS, M_HD)
    tq = q_ref.shape[0]
    qs = jnp.concatenate([q_ref[:, h * M_HD:(h + 1) * M_HD] for h in range(M_HEADS)],
                         axis=0).astype(BF16)
    s = lax.dot_general(qs, k2[...].astype(BF16), (((1,), (1,)), ((), ())),
                        preferred_element_type=F32) * (M_HD ** -0.5)
    row_h = lax.broadcasted_iota(jnp.int32, s.shape, 0) // tq
    col_h = lax.broadcasted_iota(jnp.int32, s.shape, 1) % M_HEADS
    s = jnp.where(row_h == col_h, s, -jnp.inf)
    m = jnp.max(s, axis=-1, keepdims=True)
    p = jnp.exp(s - m)
    p = p / jnp.sum(p, axis=-1, keepdims=True)
    o = jnp.dot(p.astype(BF16), v2[...].astype(BF16), preferred_element_type=F32)
    for h in range(M_HEADS):
        hs = slice(h * M_HD, (h + 1) * M_HD)
        y_ref[:, hs] = jax.nn.sigmoid(g_ref[:, hs]) * o[h * tq:(h + 1) * tq]


def _mem_attend(z, mk, mv, row0, nb, rows_per_b, tq, name, layer=None):
    nq = rows_per_b // tq
    r0 = row0 // tq

    def zspec(col):
        return pl.BlockSpec((tq, D_MODEL), lambda b, i: (r0 + b * nq + i, col // D_MODEL))

    if layer is None:
        kvspec = pl.BlockSpec((1, N_MEM, D_MODEL), lambda b, i: (b, 0, 0))
    else:
        kvspec = pl.BlockSpec((None, None, N_MEM, M_HEADS, M_HD),
                              lambda b, i: (layer, b, 0, 0, 0))
    return pl.pallas_call(
        _mem_kernel if layer is None else _mem_cache_kernel,
        grid=(nb, nq),
        in_specs=[zspec(MQ), kvspec, kvspec, zspec(GL + 3 * D_MODEL)],
        out_specs=pl.BlockSpec((tq, D_MODEL), lambda b, i: (b * nq + i, 0)),
        out_shape=jax.ShapeDtypeStruct((nb * rows_per_b, D_MODEL), F32),
        compiler_params=_cparams(("parallel", "parallel"), VMEM_LIMIT),
        name=name,
    )(z, mk, mv, z)


def _layernorm(r, g, b):
    mu = jnp.mean(r, axis=-1, keepdims=True)
    d = r - mu
    var = jnp.mean(d * d, axis=-1, keepdims=True)
    return d * lax.rsqrt(var + LN_EPS) * g + b


def _wo_ln_kernel(ya_ref, yb_ref, yc_ref, ym_ref, x_ref, wo_ref, g_ref, b_ref, rw_ref, rb_ref,
                  x1_ref, lg_ref):
    merged = ya_ref[...] + yb_ref[...] + yc_ref[...] + ym_ref[...]
    r = DN_ALPHA * x_ref[...] + jnp.dot(merged.astype(BF16), wo_ref[...], preferred_element_type=F32)
    x1 = _layernorm(r, g_ref[...], b_ref[...])
    x1_ref[...] = x1
    lg_ref[...] = jnp.dot(x1.astype(BF16), rw_ref[...], preferred_element_type=F32) + rb_ref[...]


def _wo_ln(ya, yb, yc, ym, x, wo, g, b, rw, rb, name):
    rows = x.shape[0]
    tm = 256
    row = pl.BlockSpec((tm, D_MODEL), lambda i: (i, 0))

    def full(a):
        return pl.BlockSpec(a.shape, lambda i: (0,) * a.ndim)

    return pl.pallas_call(
        _wo_ln_kernel,
        grid=(rows // tm,),
        in_specs=[row, row, row, row, row, full(wo), full(g), full(b), full(rw), full(rb)],
        out_specs=[row, pl.BlockSpec((tm, LANES), lambda i: (i, 0))],
        out_shape=[jax.ShapeDtypeStruct((rows, D_MODEL), F32),
                   jax.ShapeDtypeStruct((rows, LANES), F32)],
        compiler_params=_cparams(("parallel",), VMEM_LIMIT),
        name=name,
    )(ya, yb, yc, ym, x, wo, g, b, rw, rb)


def _route(logits):
    top_v, top_i = lax.top_k(logits, TOP_K)
    gates = jax.nn.softmax(top_v, axis=-1)
    na = N_TOK * TOP_K
    flat_e = top_i.reshape(-1).astype(jnp.int32)
    order = jnp.argsort(flat_e).astype(jnp.int32)
    inv_order = jnp.argsort(order).astype(jnp.int32)
    se = flat_e[order]
    eids = jnp.arange(N_EXP, dtype=jnp.int32)
    counts = jnp.sum((flat_e[:, None] == eids[None, :]).astype(jnp.int32), axis=0)
    padded = (counts + MOE_TB - 1) // MOE_TB * MOE_TB
    pend = jnp.cumsum(padded)
    pstart = pend - padded
    ustart = jnp.cumsum(counts) - counts
    dest = pstart[se] + jnp.arange(na, dtype=jnp.int32) - ustart[se]
    pos = dest[inv_order]
    blk_start = jnp.arange(MOE_NB, dtype=jnp.int32) * MOE_TB
    blk_e = jnp.minimum(jnp.sum((pend[None, :] <= blk_start[:, None]).astype(jnp.int32), axis=1),
                        N_EXP - 1)
    off = (blk_start - pstart[blk_e])[:, None] + jnp.arange(MOE_TB, dtype=jnp.int32)[None, :]
    src = jnp.clip(ustart[blk_e][:, None] + off, 0, na - 1)
    row_tok = jnp.where(off < counts[blk_e][:, None], order[src] // TOP_K, 0).reshape(-1)
    n_used = (pend[-1] // MOE_TB).astype(jnp.int32).reshape(1)
    return gates, row_tok, pos, blk_e, n_used


def _row_copy(src_hbm, idx, dst, slot, sem):
    return pltpu.make_async_copy(src_hbm.at[pl.ds(idx, 1)], dst.at[pl.ds(slot, 1)], sem)


def _issue_rows(src_hbm, idx_ref, n, dst, sem):
    def body(r, carry):
        _row_copy(src_hbm, idx_ref[r], dst, r, sem).start()
        return carry

    lax.fori_loop(0, n, body, 0, unroll=8)


def _wait_rows(src_hbm, n, dst, sem):
    pltpu.make_async_copy(src_hbm.at[pl.ds(0, n)], dst, sem).wait()


def _gather_kernel(nused_ref, tok_ref, tok_next_ref, x_hbm, o_ref, buf, sem):
    i = pl.program_id(0)
    nu = nused_ref[0]
    slot = i % 2

    @pl.when(i == 0)
    def _():
        _issue_rows(x_hbm, tok_ref, MOE_TB, buf.at[0], sem.at[0])

    @pl.when(i + 1 < nu)
    def _():
        _issue_rows(x_hbm, tok_next_ref, MOE_TB, buf.at[1 - slot], sem.at[1 - slot])

    @pl.when(i < nu)
    def _():
        _wait_rows(x_hbm, MOE_TB, buf.at[slot], sem.at[slot])
        o_ref[...] = buf[slot].astype(BF16)

    @pl.when(i >= nu)
    def _():
        o_ref[...] = jnp.zeros_like(o_ref)


def _moe_gather(x1, row_tok, n_used):
    return pl.pallas_call(
        _gather_kernel,
        grid_spec=pltpu.PrefetchScalarGridSpec(
            num_scalar_prefetch=1,
            grid=(MOE_NB,),
            in_specs=[pl.BlockSpec((MOE_TB,), lambda i, nu: (i,), memory_space=pltpu.SMEM),
                      pl.BlockSpec((MOE_TB,), lambda i, nu: (jnp.minimum(i + 1, MOE_NB - 1),),
                                   memory_space=pltpu.SMEM),
                      pl.BlockSpec(memory_space=pl.ANY)],
            out_specs=pl.BlockSpec((MOE_TB, D_MODEL), lambda i, nu: (i, 0)),
            scratch_shapes=[pltpu.VMEM((2, MOE_TB, D_MODEL), F32), pltpu.SemaphoreType.DMA((2,))],
        ),
        out_shape=jax.ShapeDtypeStruct((MOE_P, D_MODEL), BF16),
        compiler_params=_cparams(("arbitrary",)),
        name="moe_gather",
    )(n_used, row_tok, row_tok, x1)


def _expert_changed(be_ref, i):
    return (i == 0) | (be_ref[i] != be_ref[jnp.maximum(i - 1, 0)])


def _expert_up_kernel(be_ref, nused_ref, x_ref, wg_ref, wu_ref, bg_ref, bu_ref, h_ref,
                      wg_scr, wu_scr):
    i = pl.program_id(1)
    active = i < nused_ref[0]

    @pl.when(active & _expert_changed(be_ref, i))
    def _():
        wg_scr[...] = wg_ref[0, 0].astype(BF16)
        wu_scr[...] = wu_ref[0, 0].astype(BF16)

    @pl.when(active)
    def _():
        x = x_ref[...]
        gate = jnp.dot(x, wg_scr[...], preferred_element_type=F32) + bg_ref[0, 0]
        up = jnp.dot(x, wu_scr[...], preferred_element_type=F32) + bu_ref[0, 0]
        gate = jnp.minimum(gate, SWIGLU_LIMIT)
        up = jnp.clip(up, -SWIGLU_LIMIT, SWIGLU_LIMIT)
        h_ref[...] = ((up + 1.0) * (gate * jax.nn.sigmoid(SWIGLU_ALPHA * gate))).astype(h_ref.dtype)

    @pl.when(i >= nused_ref[0])
    def _():
        h_ref[...] = jnp.zeros_like(h_ref)


def _expert_up(xg, w_gu, b_gu, blk_e, n_used, layer):
    nf = D_FF // MOE_TF

    def used(i, nu):
        return jnp.minimum(i, nu[0] - 1)

    def wspec(rows, col0):
        return pl.BlockSpec((1, 1, rows, MOE_TF),
                            lambda j, i, be, nu: (layer, be[used(i, nu)], 0, col0 + j))

    return pl.pallas_call(
        _expert_up_kernel,
        grid_spec=pltpu.PrefetchScalarGridSpec(
            num_scalar_prefetch=2,
            grid=(nf, MOE_NB),
            in_specs=[
                pl.BlockSpec((MOE_TB, D_MODEL), lambda j, i, be, nu: (used(i, nu), 0)),
                wspec(D_MODEL, 0), wspec(D_MODEL, nf), wspec(1, 0), wspec(1, nf),
            ],
            out_specs=pl.BlockSpec((MOE_TB, MOE_TF), lambda j, i, be, nu: (i, j)),
            scratch_shapes=[pltpu.VMEM((D_MODEL, MOE_TF), BF16), pltpu.VMEM((D_MODEL, MOE_TF), BF16)],
        ),
        out_shape=jax.ShapeDtypeStruct((MOE_P, D_FF), BF16),
        compiler_params=_cparams(("arbitrary", "arbitrary"), VMEM_LIMIT),
        name="expert_up",
    )(blk_e, n_used, xg, w_gu, w_gu, b_gu, b_gu)


def _expert_down_kernel(be_ref, nused_ref, h_ref, w_ref, b_ref, y_ref, w_scr):
    i = pl.program_id(1)
    active = i < nused_ref[0]

    @pl.when(active & _expert_changed(be_ref, i))
    def _():
        w_scr[...] = w_ref[0, 0].astype(BF16)

    @pl.when(active)
    def _():
        y_ref[...] = jnp.dot(h_ref[...], w_scr[...], preferred_element_type=F32) + b_ref[0, 0]

    @pl.when(i >= nused_ref[0])
    def _():
        y_ref[...] = jnp.zeros_like(y_ref)


def _expert_down(h, w_dn, b_dn, blk_e, n_used, layer):
    nn = D_MODEL // MOE_TN

    def used(i, nu):
        return jnp.minimum(i, nu[0] - 1)

    def wspec(rows):
        return pl.BlockSpec((1, 1, rows, MOE_TN),
                            lambda j, i, be, nu: (layer, be[used(i, nu)], 0, j))

    return pl.pallas_call(
        _expert_down_kernel,
        grid_spec=pltpu.PrefetchScalarGridSpec(
            num_scalar_prefetch=2,
            grid=(nn, MOE_NB),
            in_specs=[
                pl.BlockSpec((MOE_TB, D_FF), lambda j, i, be, nu: (used(i, nu), 0)),
                wspec(D_FF), wspec(1),
            ],
            out_specs=pl.BlockSpec((MOE_TB, MOE_TN), lambda j, i, be, nu: (i, j)),
            scratch_shapes=[pltpu.VMEM((D_FF, MOE_TN), BF16)],
        ),
        out_shape=jax.ShapeDtypeStruct((MOE_P, D_MODEL), F32),
        compiler_params=_cparams(("arbitrary", "arbitrary"), VMEM_LIMIT),
        name="expert_down",
    )(blk_e, n_used, h, w_dn, b_dn)


def _combine_kernel(pos_ref, pos_next_ref, gates_ref, x1_ref, y_hbm, g_ref, b_ref, x2_ref, x2b_ref,
                    buf, sem):
    i = pl.program_id(0)
    slot = i % 2
    n = CMB_T * TOP_K

    @pl.when(i == 0)
    def _():
        _issue_rows(y_hbm, pos_ref, n, buf.at[0], sem.at[0])

    @pl.when(i + 1 < pl.num_programs(0))
    def _():
        _issue_rows(y_hbm, pos_next_ref, n, buf.at[1 - slot], sem.at[1 - slot])

    _wait_rows(y_hbm, n, buf.at[slot], sem.at[slot])
    gates = gates_ref[...]
    y = jnp.zeros((CMB_T, D_MODEL), F32)
    for k in range(TOP_K):
        y = y + buf[slot, k * CMB_T:(k + 1) * CMB_T, :] * gates[:, k:k + 1]
    x2 = _layernorm(DN_ALPHA * x1_ref[...] + y, g_ref[...], b_ref[...])
    x2_ref[...] = x2
    x2b_ref[...] = x2.astype(BF16)


def _moe_combine(pos_kmajor, gates, x1, y, g, b):
    row = pl.BlockSpec((CMB_T, D_MODEL), lambda i: (i, 0))
    vec = pl.BlockSpec((1, D_MODEL), lambda i: (0, 0))
    steps = N_TOK // CMB_T
    return pl.pallas_call(
        _combine_kernel,
        grid=(steps,),
        in_specs=[pl.BlockSpec((CMB_T * TOP_K,), lambda i: (i,), memory_space=pltpu.SMEM),
                  pl.BlockSpec((CMB_T * TOP_K,), lambda i: (jnp.minimum(i + 1, steps - 1),),
                               memory_space=pltpu.SMEM),
                  pl.BlockSpec((CMB_T, TOP_K), lambda i: (i, 0)),
                  row, pl.BlockSpec(memory_space=pl.ANY), vec, vec],
        out_specs=[row, row],
        out_shape=[jax.ShapeDtypeStruct((N_TOK, D_MODEL), F32),
                   jax.ShapeDtypeStruct((N_TOK, D_MODEL), BF16)],
        scratch_shapes=[pltpu.VMEM((2, CMB_T * TOP_K, D_MODEL), F32),
                        pltpu.SemaphoreType.DMA((2,))],
        compiler_params=_cparams(("arbitrary",)),
        name="moe_combine",
    )(pos_kmajor, pos_kmajor, gates, x1, y, g, b)


def _permute_w_in(w):
    parts, off = [], 0
    for n in REF_IN_SIZES:
        parts.append((off, n))
        off += n
    return jnp.concatenate([w[:, parts[i][0]:parts[i][0] + parts[i][1]] for i in MY_ORDER],
                           axis=1).astype(BF16)


def kernel(x_prompt, x_sample, cache_conv, state_ret, cache_swa_k, cache_swa_v, cache_mem_k, cache_mem_v, mem_prompt, w_in, conv_w, ret_gn_w, attn_sinks, w_mem_kv, w_o, ln1_g, ln1_b, router_w, router_b, w_gate_up, b_gate_up, w_down, b_down, ln2_g, ln2_b):
    pos_p = jnp.arange(SEQ, dtype=jnp.int32)
    pos_s = PAST_LEN + jnp.arange(DEC_SEQ, dtype=jnp.int32)
    ret_tabs_p = _ret_tables(pos_p, R_CHUNK)
    ret_tabs_s = _ret_tables(pos_s, DEC_SEQ)
    swa_tabs_p = _swa_tables(pos_p)
    swa_tabs_s = _swa_tables(pos_s)
    kvw = A_KV * A_HD

    xp = x_prompt.reshape(N_PROMPT, D_MODEL)
    xs = x_sample.reshape(N_SAMPLE, D_MODEL)
    xb = jnp.concatenate([xp, xs], axis=0).astype(BF16)
    mem_b = mem_prompt.reshape(BATCH * N_MEM, D_MODEL).astype(BF16)
    zeros_e = jnp.zeros((256, D_MODEL), F32)
    b_gu = b_gate_up.reshape(DEPTH, N_EXP, 1, 2 * D_FF)
    b_dn = b_down.reshape(DEPTH, N_EXP, 1, D_MODEL)
    swa_k = cache_swa_k.reshape(DEPTH, DEC_BATCH, WINDOW, kvw)
    swa_v = cache_swa_v.reshape(DEPTH, DEC_BATCH, WINDOW, kvw)

    outs = {k: [] for k in ("p_conv", "p_ret", "p_k", "p_v", "p_mk", "p_mv",
                            "s_conv", "s_ret", "s_k", "s_v")}
    for l in range(DEPTH):
        z = _matmul(xb, _permute_w_in(w_in[l]), 2304, 512)
        gn = ret_gn_w[l].reshape(1, D_MODEL)
        sinks = attn_sinks[l]

        ya_p, tail_p = _conv_branch(z, conv_w[l], zeros_e, zeros_e, 0, N_PROMPT, SEQ, False)
        yb_p, ret_p = _ret_prompt(z, ret_tabs_p, gn)
        yc_p, kr_p = _swa_prompt(z, sinks, swa_tabs_p)
        mkv = _matmul(mem_b, w_mem_kv[l].astype(BF16), 512, 512)
        mk_p = mkv[:, :D_MODEL].reshape(BATCH, N_MEM, D_MODEL)
        mv_p = mkv[:, D_MODEL:].reshape(BATCH, N_MEM, D_MODEL)
        ym_p = _mem_attend(z, mk_p, mv_p, 0, BATCH, SEQ, 512, "mem_prompt")

        ea = jnp.repeat(cache_conv[l][:, 0], DEC_SEQ, axis=0)
        eb = jnp.repeat(cache_conv[l][:, 1], DEC_SEQ, axis=0)
        ya_s, u_s = _conv_branch(z, conv_w[l], ea, eb, N_PROMPT, N_SAMPLE, DEC_SEQ, True)
        yb_s, ret_s = _ret_sample(z, ret_tabs_s, gn, state_ret, l)
        yc_s, nk_s, nv_s = _swa_sample(z, sinks, swa_tabs_s, swa_k, swa_v, l)
        ym_s = _mem_attend(z, cache_mem_k, cache_mem_v, N_PROMPT, DEC_BATCH, DEC_SEQ, DEC_SEQ,
                           "mem_sample", layer=l)

        wo_b = w_o[l].astype(BF16)
        g1 = ln1_g[l].reshape(1, D_MODEL)
        b1 = ln1_b[l].reshape(1, D_MODEL)
        rw = jnp.pad(router_w[l], ((0, 0), (0, LANES - N_EXP))).astype(BF16)
        rb = jnp.pad(router_b[l], (0, LANES - N_EXP)).reshape(1, LANES)
        x1_p, lg_p = _wo_ln(ya_p, yb_p, yc_p, ym_p, xp, wo_b, g1, b1, rw, rb, "wo_ln_prompt")
        x1_s, lg_s = _wo_ln(ya_s, yb_s, yc_s, ym_s, xs, wo_b, g1, b1, rw, rb, "wo_ln_sample")
        x1 = jnp.concatenate([x1_p, x1_s], axis=0)
        logits = jnp.concatenate([lg_p, lg_s], axis=0)[:, :N_EXP]

        gates, row_tok, pos, blk_e, n_used = _route(logits)
        xg = _moe_gather(x1, row_tok, n_used)
        h = _expert_up(xg, w_gate_up, b_gu, blk_e, n_used, l)
        y = _expert_down(h, w_down, b_dn, blk_e, n_used, l)
        pos_k = pos.reshape(N_TOK // CMB_T, CMB_T, TOP_K).transpose(0, 2, 1).reshape(-1)
        x2, xb = _moe_combine(pos_k, gates, x1, y, ln2_g[l].reshape(1, D_MODEL),
                              ln2_b[l].reshape(1, D_MODEL))
        xp, xs = x2[:N_PROMPT], x2[N_PROMPT:]

        tail_p = tail_p.reshape(BATCH, SEQ // 256, SUBLANES, D_MODEL)
        outs["p_conv"].append(tail_p[:, -1, -(CONV_W - 1):])
        outs["p_ret"].append(ret_p)
        outs["p_k"].append(kr_p.reshape(BATCH, SEQ, A_KV, A_HD)[:, -WINDOW:])
        outs["p_v"].append(z[:N_PROMPT, AV:AV + kvw].reshape(BATCH, SEQ, A_KV, A_HD)[:, -WINDOW:])
        outs["p_mk"].append(mk_p.reshape(BATCH, N_MEM, M_HEADS, M_HD))
        outs["p_mv"].append(mv_p.reshape(BATCH, N_MEM, M_HEADS, M_HD))
        outs["s_conv"].append(u_s.reshape(DEC_BATCH, DEC_SEQ, D_MODEL)[:, -(CONV_W - 1):])
        outs["s_ret"].append(ret_s)
        outs["s_k"].append(nk_s.reshape(DEC_BATCH, WINDOW, A_KV, A_HD))
        outs["s_v"].append(nv_s.reshape(DEC_BATCH, WINDOW, A_KV, A_HD))

    return (xp.reshape(BATCH, SEQ, D_MODEL), xs.reshape(DEC_BATCH, DEC_SEQ, D_MODEL),
            jnp.stack(outs["p_conv"]), jnp.stack(outs["p_ret"]), jnp.stack(outs["p_k"]),
            jnp.stack(outs["p_v"]), jnp.stack(outs["p_mk"]), jnp.stack(outs["p_mv"]),
            jnp.stack(outs["s_conv"]), jnp.stack(outs["s_ret"]), jnp.stack(outs["s_k"]),
            jnp.stack(outs["s_v"]))
```

```python
import functools

import jax
import jax.numpy as jnp
from jax import lax
from jax.experimental import pallas as pl
from jax.experimental.pallas import tpu as pltpu

F32 = jnp.float32
BF16 = jnp.bfloat16

D_MODEL = 2048
BATCH = 2
SEQ = 4096
DEPTH = 2
DEC_BATCH = 128
DEC_SEQ = 8
PAST_LEN = 8192
N_PROMPT = BATCH * SEQ
N_SAMPLE = DEC_BATCH * DEC_SEQ
N_TOK = N_PROMPT + N_SAMPLE

CONV_W = 3
R_HEADS = 8
R_DK = 128
R_DV = D_MODEL // R_HEADS
R_CHUNK = 128
R_THETA = 10000.0
A_HEADS = 32
A_KV = 4
A_HD = D_MODEL // A_HEADS
A_GROUP = A_HEADS // A_KV
WINDOW = 128
ROPE_DIMS = A_HD // 4
ROPE_THETA = 500000.0
N_MEM = 256
M_HEADS = 4
M_HD = D_MODEL // M_HEADS
N_EXP = 32
TOP_K = 4
D_FF = D_MODEL
SWIGLU_LIMIT = 7.0
SWIGLU_ALPHA = 1.702
LN_EPS = 1e-5
GN_EPS = 1e-6
DN_ALPHA = (2 * DEPTH) ** 0.25

REF_IN_SIZES = (D_MODEL, D_MODEL, D_MODEL, R_HEADS * R_DK, R_HEADS * R_DK, D_MODEL, D_MODEL,
                D_MODEL, A_KV * A_HD, A_KV * A_HD, D_MODEL, 4 * D_MODEL)
D_IN = sum(REF_IN_SIZES)
MY_ORDER = (11, 0, 1, 2, 5, 6, 7, 10, 3, 4, 8, 9)
GL, HC, BC, CC, RV, RG, AQ, MQ, RQ, RK, AK, AV = (
    0, 8192, 10240, 12288, 14336, 16384, 18432, 20480, 22528, 23552, 24576, 24832)

LANES = 128
SUBLANES = 8
VMEM_LIMIT = 56 * 1024 * 1024

MOE_TB = 256
MOE_NB = N_TOK * TOP_K // MOE_TB + N_EXP
MOE_P = MOE_NB * MOE_TB
MOE_TF = 512
MOE_TN = 1024
CMB_T = 64


def _cparams(sem, vmem=None):
    return pltpu.CompilerParams(dimension_semantics=sem, vmem_limit_bytes=vmem)


def _mm_kernel(x_ref, w_ref, o_ref):
    o_ref[...] = jnp.dot(x_ref[...], w_ref[...], preferred_element_type=F32).astype(o_ref.dtype)


def _matmul(x, w, tm, tn, out_dtype=F32):
    m, k = x.shape
    n = w.shape[1]
    return pl.pallas_call(
        _mm_kernel,
        grid=(m // tm, n // tn),
        in_specs=[pl.BlockSpec((tm, k), lambda i, j: (i, 0)),
                  pl.BlockSpec((k, tn), lambda i, j: (0, j))],
        out_specs=pl.BlockSpec((tm, tn), lambda i, j: (i, j)),
        out_shape=jax.ShapeDtypeStruct((m, n), out_dtype),
        compiler_params=_cparams(("parallel", "parallel"), VMEM_LIMIT),
        name="matmul",
    )(x, w)


def _conv_kernel(seq_len, full_u, hc_ref, cc_ref, bc_ref, g_ref, hh_ref, ch_ref, ea_ref, eb_ref,
                 w_ref, ya_ref, tail_ref):
    i = pl.program_id(0)
    tm = hc_ref.shape[0]
    u = cc_ref[...] * hc_ref[...]
    ue = jnp.concatenate([ch_ref[...] * hh_ref[...], u], axis=0)
    p1 = pltpu.roll(ue, 1, 0)[SUBLANES:]
    p2 = pltpu.roll(ue, 2, 0)[SUBLANES:]
    t = (i * tm + lax.broadcasted_iota(jnp.int32, (tm, 1), 0)) % seq_len
    ea = ea_ref[...]
    eb = eb_ref[...]
    p1 = jnp.where(t == 0, eb, p1)
    p2 = jnp.where(t == 0, ea, jnp.where(t == 1, eb, p2))
    w = w_ref[...]
    y = p2 * w[0:1] + p1 * w[1:2] + u * w[2:3]
    ya_ref[...] = jax.nn.sigmoid(g_ref[...]) * (bc_ref[...] * y)
    tail_ref[...] = u if full_u else u[tm - SUBLANES:]


def _conv_branch(z, conv_w, ea, eb, row0, rows, seq_len, sample):
    tm, tc = 256, 512
    r0 = row0 // tm
    nj = D_MODEL // tc

    def zspec(col):
        return pl.BlockSpec((tm, tc), lambda i, j: (r0 + i, col // tc + j))

    def halo(col):
        return pl.BlockSpec(
            (SUBLANES, tc),
            lambda i, j: (jnp.maximum((r0 + i) * (tm // SUBLANES) - 1, 0), col // tc + j))

    if sample:
        espec = pl.BlockSpec((tm, tc), lambda i, j: (i, j))
        tail_spec = pl.BlockSpec((tm, tc), lambda i, j: (i, j))
        tail_shape = jax.ShapeDtypeStruct((rows, D_MODEL), F32)
    else:
        espec = pl.BlockSpec((tm, tc), lambda i, j: (0, j))
        tail_spec = pl.BlockSpec((SUBLANES, tc), lambda i, j: (i, j))
        tail_shape = jax.ShapeDtypeStruct((rows // tm * SUBLANES, D_MODEL), F32)
    return pl.pallas_call(
        functools.partial(_conv_kernel, seq_len, sample),
        grid=(rows // tm, nj),
        in_specs=[zspec(HC), zspec(CC), zspec(BC), zspec(GL), halo(HC), halo(CC), espec, espec,
                  pl.BlockSpec((CONV_W, tc), lambda i, j: (0, j))],
        out_specs=[pl.BlockSpec((tm, tc), lambda i, j: (i, j)), tail_spec],
        out_shape=[jax.ShapeDtypeStruct((rows, D_MODEL), F32), tail_shape],
        compiler_params=_cparams(("parallel", "parallel"), VMEM_LIMIT),
        name="conv_sample" if sample else "conv_prompt",
    )(z, z, z, z, z, z, ea, eb, conv_w)


def _rope_full(x, cos, sin):
    return x * cos + pltpu.roll(x, R_DK // 2, 1) * sin


def _bf16_exact(x):
    return x.astype(BF16).astype(F32)


def _group_norm_gate(o, gn, rg, g):
    mu = jnp.mean(o, axis=-1, keepdims=True)
    d = o - mu
    var = jnp.mean(d * d, axis=-1, keepdims=True)
    yn = d * lax.rsqrt(var + GN_EPS) * gn
    return jax.nn.sigmoid(g) * ((rg * jax.nn.sigmoid(rg)) * yn)


def _ret_prompt_kernel(q_ref, k_ref, v_ref, rg_ref, g_ref, cos_ref, sin_ref, dmask_ref, qdec_ref,
                       kdec_ref, cdec_ref, gn_ref, y_ref, sfin_ref, s_scr):
    c = pl.program_id(1)

    @pl.when(c == 0)
    def _():
        s_scr[...] = jnp.zeros_like(s_scr)

    cos = cos_ref[...]
    sin = sin_ref[...]
    for h in range(R_HEADS):
        qs = slice(h * R_DK, (h + 1) * R_DK)
        vs = slice(h * R_DV, (h + 1) * R_DV)
        q = _rope_full(q_ref[:, qs], cos, sin)
        k = _rope_full(k_ref[:, qs], cos, sin) * (R_DK ** -0.5)
        vb = v_ref[:, vs].astype(BF16)
        s = s_scr[h]
        att = lax.dot_general(q.astype(BF16), k.astype(BF16), (((1,), (1,)), ((), ())),
                              preferred_element_type=F32) * dmask_ref[h]
        o = (jnp.dot(att.astype(BF16), vb, preferred_element_type=F32)
             + jnp.dot((q * qdec_ref[h]).astype(BF16), s.astype(BF16), preferred_element_type=F32))
        kd = (k * kdec_ref[h]).astype(BF16)
        s_new = cdec_ref[h] * s + lax.dot_general(kd, vb, (((0,), (0,)), ((), ())),
                                                  preferred_element_type=F32)
        s_scr[h] = s_new
        sfin_ref[0, h] = s_new
        y_ref[:, vs] = _group_norm_gate(o, gn_ref[:, vs], rg_ref[:, vs], g_ref[:, vs])


def _ret_prompt(z, tabs, gn_w):
    cos, sin, dmask, qdec, kdec, cdec = tabs
    nc = SEQ // R_CHUNK

    def zspec(col, width):
        return pl.BlockSpec((R_CHUNK, width), lambda b, c: (b * nc + c, col // width))

    def full(a):
        return pl.BlockSpec(a.shape, lambda b, c: (0,) * a.ndim)

    tspec = pl.BlockSpec((R_CHUNK, R_DK), lambda b, c: (c, 0))
    return pl.pallas_call(
        _ret_prompt_kernel,
        grid=(BATCH, nc),
        in_specs=[zspec(RQ, R_HEADS * R_DK), zspec(RK, R_HEADS * R_DK), zspec(RV, D_MODEL),
                  zspec(RG, D_MODEL), zspec(GL + D_MODEL, D_MODEL), tspec, tspec,
                  full(dmask), full(qdec), full(kdec), full(cdec), full(gn_w)],
        out_specs=[pl.BlockSpec((R_CHUNK, D_MODEL), lambda b, c: (b * nc + c, 0)),
                   pl.BlockSpec((1, R_HEADS, R_DK, R_DV), lambda b, c: (b, 0, 0, 0))],
        out_shape=[jax.ShapeDtypeStruct((N_PROMPT, D_MODEL), F32),
                   jax.ShapeDtypeStruct((BATCH, R_HEADS, R_DK, R_DV), F32)],
        scratch_shapes=[pltpu.VMEM((R_HEADS, R_DK, R_DV), F32)],
        compiler_params=_cparams(("parallel", "arbitrary"), VMEM_LIMIT),
        name="retention_prompt",
    )(z, z, z, z, z, cos, sin, dmask, qdec, kdec, cdec, gn_w)


def _ret_sample_kernel(q_ref, k_ref, v_ref, rg_ref, g_ref, cos_ref, sin_ref, dmask_ref, qdec_ref,
                       kdec_ref, cdec_ref, gn_ref, s_ref, y_ref, snew_ref):
    cos = cos_ref[...]
    sin = sin_ref[...]
    for h in range(R_HEADS):
        qs = slice(h * R_DK, (h + 1) * R_DK)
        vs = slice(h * R_DV, (h + 1) * R_DV)
        q = _rope_full(q_ref[:, qs], cos, sin)
        k = _rope_full(k_ref[:, qs], cos, sin) * (R_DK ** -0.5)
        v = _bf16_exact(v_ref[:, vs])
        s = s_ref[0, 0, h]
        att = lax.dot_general(_bf16_exact(q), _bf16_exact(k), (((1,), (1,)), ((), ())),
                              preferred_element_type=F32) * dmask_ref[h]
        o = (jnp.dot(_bf16_exact(att), v, preferred_element_type=F32)
             + jnp.dot((q * qdec_ref[h]).astype(BF16), s.astype(BF16), preferred_element_type=F32))
        kd = _bf16_exact(k * kdec_ref[h])
        snew_ref[0, h] = cdec_ref[h] * s + lax.dot_general(
            kd, v, (((0,), (0,)), ((), ())), preferred_element_type=F32)
        y_ref[:, vs] = _group_norm_gate(o, gn_ref[:, vs], rg_ref[:, vs], g_ref[:, vs])


def _ret_sample(z, tabs, gn_w, state, layer):
    cos, sin, dmask, qdec, kdec, cdec = tabs
    r0 = N_PROMPT // DEC_SEQ

    def zspec(col, width):
        return pl.BlockSpec((DEC_SEQ, width), lambda b: (r0 + b, col // width))

    def full(a):
        return pl.BlockSpec(a.shape, lambda b: (0,) * a.ndim)

    sspec = pl.BlockSpec((1, R_HEADS, R_DK, R_DV), lambda b: (b, 0, 0, 0))
    sin_spec = pl.BlockSpec((1, 1, R_HEADS, R_DK, R_DV), lambda b: (layer, b, 0, 0, 0))
    return pl.pallas_call(
        _ret_sample_kernel,
        grid=(DEC_BATCH,),
        in_specs=[zspec(RQ, R_HEADS * R_DK), zspec(RK, R_HEADS * R_DK), zspec(RV, D_MODEL),
                  zspec(RG, D_MODEL), zspec(GL + D_MODEL, D_MODEL),
                  full(cos), full(sin), full(dmask), full(qdec), full(kdec), full(cdec), full(gn_w),
                  sin_spec],
        out_specs=[pl.BlockSpec((DEC_SEQ, D_MODEL), lambda b: (b, 0)), sspec],
        out_shape=[jax.ShapeDtypeStruct((N_SAMPLE, D_MODEL), F32),
                   jax.ShapeDtypeStruct(state.shape[1:], F32)],
        compiler_params=_cparams(("parallel",)),
        name="retention_sample",
    )(z, z, z, z, z, cos, sin, dmask, qdec, kdec, cdec, gn_w, state)


def _ret_tables(pos, chunk):
    half = R_DK // 2
    inv = R_THETA ** (-jnp.arange(half, dtype=F32) / half)
    ang = pos.astype(F32)[:, None] * inv[None, :]
    cos = jnp.concatenate([jnp.cos(ang), jnp.cos(ang)], axis=-1)
    sin = jnp.concatenate([-jnp.sin(ang), jnp.sin(ang)], axis=-1)
    lg = jnp.log(1.0 - 2.0 ** (-5.0 - jnp.arange(R_HEADS, dtype=F32)))
    i = jnp.arange(chunk, dtype=F32)
    diff = i[:, None] - i[None, :]
    dmask = jnp.where(diff >= 0, jnp.exp(lg[:, None, None] * jnp.maximum(diff, 0.0)), 0.0)
    qdec = jnp.broadcast_to(jnp.exp(lg[:, None] * (i + 1.0))[:, :, None], (R_HEADS, chunk, R_DK))
    kdec = jnp.broadcast_to(jnp.exp(lg[:, None] * (chunk - 1.0 - i))[:, :, None],
                            (R_HEADS, chunk, R_DK))
    cdec = jnp.broadcast_to(jnp.exp(lg * chunk)[:, None, None], (R_HEADS, 1, R_DV))
    return cos, sin, dmask.astype(F32), qdec.astype(F32), kdec.astype(F32), cdec.astype(F32)


def _swa_tables(pos):
    half = ROPE_DIMS // 2
    inv = ROPE_THETA ** (-jnp.arange(half, dtype=F32) / half)
    ang = pos.astype(F32)[:, None] * inv[None, :]
    cos, sin = jnp.cos(ang), jnp.sin(ang)
    t = pos.shape[0]
    ones = jnp.ones((t, A_HD - ROPE_DIMS), F32)
    zeros = jnp.zeros((t, A_HD - ROPE_DIMS), F32)
    zh = jnp.zeros((t, half), F32)
    c = jnp.concatenate([cos, cos, ones], axis=-1)
    s_lo = jnp.concatenate([-sin, zh, zeros], axis=-1)
    s_hi = jnp.concatenate([zh, sin, zeros], axis=-1)
    rep = LANES // A_HD
    return jnp.tile(c, (1, rep)), jnp.tile(s_lo, (1, rep)), jnp.tile(s_hi, (1, rep))


def _rope_partial(x, c, s_lo, s_hi):
    half = ROPE_DIMS // 2
    return x * c + pltpu.roll(x, LANES - half, 1) * s_lo + pltpu.roll(x, half, 1) * s_hi


def _dup_head(chunk, odd):
    lane = lax.broadcasted_iota(jnp.int32, chunk.shape, 1)
    sw = pltpu.roll(chunk, A_HD, 1)
    if odd:
        return jnp.where(lane < A_HD, sw, chunk)
    return jnp.where(lane < A_HD, chunk, sw)


def _swa_core(sink_ref, q_ref, g_ref, y_ref, kp, kc, vp, vc, tabs_q, has_prev, tq, chunks_per_dot):
    cq, slq, shq = tabs_q
    lane_q = lax.broadcasted_iota(jnp.int32, (tq, LANES), 1)
    n_chunks = A_GROUP * A_HD // LANES
    for h in range(A_KV):
        kchunk = slice((h // 2) * LANES, (h // 2 + 1) * LANES)
        kkp = _dup_head(kp[:, kchunk], h % 2).astype(BF16)
        kkc = _dup_head(kc[:, kchunk], h % 2).astype(BF16)
        vvp = _dup_head(vp[:, kchunk], h % 2).astype(BF16)
        vvc = _dup_head(vc[:, kchunk], h % 2).astype(BF16)
        for c0 in range(0, n_chunks, chunks_per_dot):
            pieces, sinks = [], []
            for ci in range(c0, c0 + chunks_per_dot):
                col = h * A_GROUP * A_HD + ci * LANES
                qc = _rope_partial(q_ref[:, col:col + LANES], cq, slq, shq) * (A_HD ** -0.5)
                for half in range(2):
                    keep = (lane_q < A_HD) if half == 0 else (lane_q >= A_HD)
                    pieces.append(jnp.where(keep, qc, 0.0).astype(BF16))
                    head = h * A_GROUP + 2 * ci + half
                    sinks.append(jnp.full((tq, 1), sink_ref[head], F32))
            qs = jnp.concatenate(pieces, axis=0)
            sink = jnp.concatenate(sinks, axis=0)
            r = qs.shape[0]
            nt = (((1,), (1,)), ((), ()))
            sp = lax.dot_general(qs, kkp, nt, preferred_element_type=F32)
            sc = lax.dot_general(qs, kkc, nt, preferred_element_type=F32)
            t = lax.broadcasted_iota(jnp.int32, (r, WINDOW), 0) % tq
            j = lax.broadcasted_iota(jnp.int32, (r, WINDOW), 1)
            sp = jnp.where((j >= t) & has_prev, sp, -jnp.inf)
            sc = jnp.where(j <= t, sc, -jnp.inf)
            m = jnp.maximum(jnp.max(jnp.maximum(sp, sc), axis=-1, keepdims=True), sink)
            pp = jnp.exp(sp - m)
            pc = jnp.exp(sc - m)
            den = jnp.sum(pp + pc, axis=-1, keepdims=True) + jnp.exp(sink - m)
            inv = 1.0 / den
            o = (jnp.dot((pp * inv).astype(BF16), vvp, preferred_element_type=F32)
                 + jnp.dot((pc * inv).astype(BF16), vvc, preferred_element_type=F32))
            for n, ci in enumerate(range(c0, c0 + chunks_per_dot)):
                col = h * A_GROUP * A_HD + ci * LANES
                oa = o[(2 * n) * tq:(2 * n + 1) * tq]
                ob = o[(2 * n + 1) * tq:(2 * n + 2) * tq]
                oc = jnp.where(lane_q < A_HD, oa, ob)
                y_ref[:, col:col + LANES] = jax.nn.sigmoid(g_ref[:, col:col + LANES]) * oc


def _rope_kv(k_ref, ck, slk, shk):
    parts = [_rope_partial(k_ref[:, c * LANES:(c + 1) * LANES], ck, slk, shk)
             for c in range(A_KV * A_HD // LANES)]
    return jnp.concatenate(parts, axis=-1)


def _swa_prompt_kernel(sink_ref, q_ref, k_ref, v_ref, g_ref, c_ref, sl_ref, sh_ref,
                       y_ref, kr_ref, kp_scr, vp_scr):
    n = pl.program_id(1)

    @pl.when(n == 0)
    def _():
        kp_scr[...] = jnp.zeros_like(kp_scr)
        vp_scr[...] = jnp.zeros_like(vp_scr)

    tabs = (c_ref[...], sl_ref[...], sh_ref[...])
    kc = _rope_kv(k_ref, *tabs)
    vc = v_ref[...]
    kr_ref[...] = kc
    _swa_core(sink_ref, q_ref, g_ref, y_ref, kp_scr[...], kc, vp_scr[...], vc, tabs, n > 0,
              WINDOW, 1)
    kp_scr[...] = kc
    vp_scr[...] = vc


def _swa_prompt(z, sinks, tabs):
    nb = SEQ // WINDOW
    kvw = A_KV * A_HD

    def zspec(col, width):
        return pl.BlockSpec((WINDOW, width), lambda b, n: (b * nb + n, col // width))

    tspec = pl.BlockSpec((WINDOW, LANES), lambda b, n: (n, 0))
    return pl.pallas_call(
        _swa_prompt_kernel,
        grid=(BATCH, nb),
        in_specs=[pl.BlockSpec(memory_space=pltpu.SMEM),
                  zspec(AQ, D_MODEL), zspec(AK, kvw), zspec(AV, kvw), zspec(GL + 2 * D_MODEL, D_MODEL),
                  tspec, tspec, tspec],
        out_specs=[pl.BlockSpec((WINDOW, D_MODEL), lambda b, n: (b * nb + n, 0)),
                   pl.BlockSpec((WINDOW, kvw), lambda b, n: (b * nb + n, 0))],
        out_shape=[jax.ShapeDtypeStruct((N_PROMPT, D_MODEL), F32),
                   jax.ShapeDtypeStruct((N_PROMPT, kvw), F32)],
        scratch_shapes=[pltpu.VMEM((WINDOW, kvw), F32), pltpu.VMEM((WINDOW, kvw), F32)],
        compiler_params=_cparams(("parallel", "arbitrary"), VMEM_LIMIT),
        name="swa_prompt",
    )(sinks, z, z, z, z, *tabs)


def _swa_sample_kernel(sink_ref, q_ref, k_ref, v_ref, g_ref, c_ref, sl_ref, sh_ref, ck_ref, cv_ref,
                       y_ref, nk_ref, nv_ref):
    tabs = (c_ref[...], sl_ref[...], sh_ref[...])
    kn = _rope_kv(k_ref, *tabs)
    vn = v_ref[...]
    pad = jnp.zeros((WINDOW - DEC_SEQ, A_KV * A_HD), F32)
    kp = ck_ref[0, 0]
    vp = cv_ref[0, 0]
    _swa_core(sink_ref, q_ref, g_ref, y_ref, kp, jnp.concatenate([kn, pad], axis=0),
              vp, jnp.concatenate([vn, pad], axis=0), tabs, True, DEC_SEQ,
              A_GROUP * A_HD // LANES)
    nk_ref[0] = jnp.concatenate([kp[DEC_SEQ:], kn], axis=0)
    nv_ref[0] = jnp.concatenate([vp[DEC_SEQ:], vn], axis=0)


def _swa_sample(z, sinks, tabs, cache_k, cache_v, layer):
    kvw = A_KV * A_HD
    r0 = N_PROMPT // DEC_SEQ

    def zspec(col, width):
        return pl.BlockSpec((DEC_SEQ, width), lambda b: (r0 + b, col // width))

    tspec = pl.BlockSpec((DEC_SEQ, LANES), lambda b: (0, 0))
    cspec = pl.BlockSpec((1, WINDOW, kvw), lambda b: (b, 0, 0))
    cin_spec = pl.BlockSpec((1, 1, WINDOW, kvw), lambda b: (layer, b, 0, 0))
    return pl.pallas_call(
        _swa_sample_kernel,
        grid=(DEC_BATCH,),
        in_specs=[pl.BlockSpec(memory_space=pltpu.SMEM),
                  zspec(AQ, D_MODEL), zspec(AK, kvw), zspec(AV, kvw), zspec(GL + 2 * D_MODEL, D_MODEL),
                  tspec, tspec, tspec, cin_spec, cin_spec],
        out_specs=[pl.BlockSpec((DEC_SEQ, D_MODEL), lambda b: (b, 0)), cspec, cspec],
        out_shape=[jax.ShapeDtypeStruct((N_SAMPLE, D_MODEL), F32),
                   jax.ShapeDtypeStruct((DEC_BATCH, WINDOW, kvw), F32),
                   jax.ShapeDtypeStruct((DEC_BATCH, WINDOW, kvw), F32)],
        compiler_params=_cparams(("parallel",)),
        name="swa_sample",
    )(sinks, z, z, z, z, *tabs, cache_k, cache_v)


def _mem_kernel(q_ref, k_ref, v_ref, g_ref, y_ref):
    for h in range(M_HEADS):
        hs = slice(h * M_HD, (h + 1) * M_HD)
        q = q_ref[:, hs].astype(BF16)
        k = k_ref[0, :, hs].astype(BF16)
        v = v_ref[0, :, hs].astype(BF16)
        s = lax.dot_general(q, k, (((1,), (1,)), ((), ())), preferred_element_type=F32) * (M_HD ** -0.5)
        m = jnp.max(s, axis=-1, keepdims=True)
        p = jnp.exp(s - m)
        p = p / jnp.sum(p, axis=-1, keepdims=True)
        o = jnp.dot(p.astype(BF16), v, preferred_element_type=F32)
        y_ref[:, hs] = jax.nn.sigmoid(g_ref[:, hs]) * o


def _mem_cache_kernel(q_ref, k_ref, v_ref, g_ref, y_ref):
    k2 = k_ref.reshape(N_MEM * M_HEADS, M_HD)
    v2 = v_ref.reshape(N_MEM * M_HEADS, M_HD)
    tq = q_ref.shape[0]
    qs = jnp.concatenate([q_ref[:, h * M_HD:(h + 1) * M_HD] for h in range(M_HEADS)],
                         axis=0).astype(BF16)
    s = lax.dot_general(qs, k2[...].astype(BF16), (((1,), (1,)), ((), ())),
                        preferred_element_type=F32) * (M_HD ** -0.5)
    row_h = lax.broadcasted_iota(jnp.int32, s.shape, 0) // tq
    col_h = lax.broadcasted_iota(jnp.int32, s.shape, 1) % M_HEADS
    s = jnp.where(row_h == col_h, s, -jnp.inf)
    m = jnp.max(s, axis=-1, keepdims=True)
    p = jnp.exp(s - m)
    p = p / jnp.sum(p, axis=-1, keepdims=True)
    o = jnp.dot(p.astype(BF16), v2[...].astype(BF16), preferred_element_type=F32)
    for h in range(M_HEADS):
        hs = slice(h * M_HD, (h + 1) * M_HD)
        y_ref[:, hs] = jax.nn.sigmoid(g_ref[:, hs]) * o[h * tq:(h + 1) * tq]


def _mem_attend(z, mk, mv, row0, nb, rows_per_b, tq, name, layer=None):
    nq = rows_per_b // tq
    r0 = row0 // tq

    def zspec(col):
        return pl.BlockSpec((tq, D_MODEL), lambda b, i: (r0 + b * nq + i, col // D_MODEL))

    if layer is None:
        kvspec = pl.BlockSpec((1, N_MEM, D_MODEL), lambda b, i: (b, 0, 0))
    else:
        kvspec = pl.BlockSpec((None, None, N_MEM, M_HEADS, M_HD),
                              lambda b, i: (layer, b, 0, 0, 0))
    return pl.pallas_call(
        _mem_kernel if layer is None else _mem_cache_kernel,
        grid=(nb, nq),
        in_specs=[zspec(MQ), kvspec, kvspec, zspec(GL + 3 * D_MODEL)],
        out_specs=pl.BlockSpec((tq, D_MODEL), lambda b, i: (b * nq + i, 0)),
        out_shape=jax.ShapeDtypeStruct((nb * rows_per_b, D_MODEL), F32),
        compiler_params=_cparams(("parallel", "parallel"), VMEM_LIMIT),
        name=name,
    )(z, mk, mv, z)


def _layernorm(r, g, b):
    mu = jnp.mean(r, axis=-1, keepdims=True)
    d = r - mu
    var = jnp.mean(d * d, axis=-1, keepdims=True)
    return d * lax.rsqrt(var + LN_EPS) * g + b


def _wo_ln_kernel(ya_ref, yb_ref, yc_ref, ym_ref, x_ref, wo_ref, g_ref, b_ref, rw_ref, rb_ref,
                  x1_ref, lg_ref):
    merged = ya_ref[...] + yb_ref[...] + yc_ref[...] + ym_ref[...]
    r = DN_ALPHA * x_ref[...] + jnp.dot(merged.astype(BF16), wo_ref[...], preferred_element_type=F32)
    x1 = _layernorm(r, g_ref[...], b_ref[...])
    x1_ref[...] = x1
    lg_ref[...] = jnp.dot(x1.astype(BF16), rw_ref[...], preferred_element_type=F32) + rb_ref[...]


def _wo_ln(ya, yb, yc, ym, x, wo, g, b, rw, rb, name):
    rows = x.shape[0]
    tm = 256
    row = pl.BlockSpec((tm, D_MODEL), lambda i: (i, 0))

    def full(a):
        return pl.BlockSpec(a.shape, lambda i: (0,) * a.ndim)

    return pl.pallas_call(
        _wo_ln_kernel,
        grid=(rows // tm,),
        in_specs=[row, row, row, row, row, full(wo), full(g), full(b), full(rw), full(rb)],
        out_specs=[row, pl.BlockSpec((tm, LANES), lambda i: (i, 0))],
        out_shape=[jax.ShapeDtypeStruct((rows, D_MODEL), F32),
                   jax.ShapeDtypeStruct((rows, LANES), F32)],
        compiler_params=_cparams(("parallel",), VMEM_LIMIT),
        name=name,
    )(ya, yb, yc, ym, x, wo, g, b, rw, rb)


def _route(logits):
    top_v, top_i = lax.top_k(logits, TOP_K)
    gates = jax.nn.softmax(top_v, axis=-1)
    na = N_TOK * TOP_K
    flat_e = top_i.reshape(-1).astype(jnp.int32)
    order = jnp.argsort(flat_e).astype(jnp.int32)
    inv_order = jnp.argsort(order).astype(jnp.int32)
    se = flat_e[order]
    eids = jnp.arange(N_EXP, dtype=jnp.int32)
    counts = jnp.sum((flat_e[:, None] == eids[None, :]).astype(jnp.int32), axis=0)
    padded = (counts + MOE_TB - 1) // MOE_TB * MOE_TB
    pend = jnp.cumsum(padded)
    pstart = pend - padded
    ustart = jnp.cumsum(counts) - counts
    dest = pstart[se] + jnp.arange(na, dtype=jnp.int32) - ustart[se]
    pos = dest[inv_order]
    blk_start = jnp.arange(MOE_NB, dtype=jnp.int32) * MOE_TB
    blk_e = jnp.minimum(jnp.sum((pend[None, :] <= blk_start[:, None]).astype(jnp.int32), axis=1),
                        N_EXP - 1)
    off = (blk_start - pstart[blk_e])[:, None] + jnp.arange(MOE_TB, dtype=jnp.int32)[None, :]
    src = jnp.clip(ustart[blk_e][:, None] + off, 0, na - 1)
    row_tok = jnp.where(off < counts[blk_e][:, None], order[src] // TOP_K, 0).reshape(-1)
    n_used = (pend[-1] // MOE_TB).astype(jnp.int32).reshape(1)
    has = counts > 0
    later = jnp.where((eids[None, :] > eids[:, None]) & has[None, :], eids[None, :], N_EXP)
    first_e = jnp.min(jnp.where(has, eids, N_EXP))
    nxt_of = jnp.min(later, axis=1)
    nxt_e = jnp.where(nxt_of < N_EXP, nxt_of, first_e)[blk_e].astype(jnp.int32)
    return gates, row_tok, pos, blk_e, nxt_e, n_used


def _row_copy(src_hbm, idx, dst, slot, sem):
    return pltpu.make_async_copy(src_hbm.at[pl.ds(idx, 1)], dst.at[pl.ds(slot, 1)], sem)


def _issue_rows(src_hbm, idx_ref, n, dst, sem):
    def body(r, carry):
        _row_copy(src_hbm, idx_ref[r], dst, r, sem).start()
        return carry

    lax.fori_loop(0, n, body, 0, unroll=8)


def _wait_rows(src_hbm, n, dst, sem):
    pltpu.make_async_copy(src_hbm.at[pl.ds(0, n)], dst, sem).wait()


def _gather_kernel(nused_ref, tok_ref, tok_next_ref, x_hbm, o_ref, buf, sem):
    i = pl.program_id(0)
    nu = nused_ref[0]
    slot = i % 2

    @pl.when(i == 0)
    def _():
        _issue_rows(x_hbm, tok_ref, MOE_TB, buf.at[0], sem.at[0])

    @pl.when(i + 1 < nu)
    def _():
        _issue_rows(x_hbm, tok_next_ref, MOE_TB, buf.at[1 - slot], sem.at[1 - slot])

    @pl.when(i < nu)
    def _():
        _wait_rows(x_hbm, MOE_TB, buf.at[slot], sem.at[slot])
        o_ref[...] = buf[slot].astype(BF16)

    @pl.when(i >= nu)
    def _():
        o_ref[...] = jnp.zeros_like(o_ref)


def _moe_gather(x1, row_tok, n_used):
    return pl.pallas_call(
        _gather_kernel,
        grid_spec=pltpu.PrefetchScalarGridSpec(
            num_scalar_prefetch=1,
            grid=(MOE_NB,),
            in_specs=[pl.BlockSpec((MOE_TB,), lambda i, nu: (i,), memory_space=pltpu.SMEM),
                      pl.BlockSpec((MOE_TB,), lambda i, nu: (jnp.minimum(i + 1, MOE_NB - 1),),
                                   memory_space=pltpu.SMEM),
                      pl.BlockSpec(memory_space=pl.ANY)],
            out_specs=pl.BlockSpec((MOE_TB, D_MODEL), lambda i, nu: (i, 0)),
            scratch_shapes=[pltpu.VMEM((2, MOE_TB, D_MODEL), F32), pltpu.SemaphoreType.DMA((2,))],
        ),
        out_shape=jax.ShapeDtypeStruct((MOE_P, D_MODEL), BF16),
        compiler_params=_cparams(("arbitrary",)),
        name="moe_gather",
    )(n_used, row_tok, row_tok, x1)


def _expert_changed(be_ref, i):
    return (i == 0) | (be_ref[i] != be_ref[jnp.maximum(i - 1, 0)])


def _stream_weight_tile(be_ref, nxt_ref, nused_ref, w_hbm, layer, col_starts, width, stage, dsts,
                        cnt_ref, sem):
    j = pl.program_id(0)
    i = pl.program_id(1)
    nu = nused_ref[0]

    def copies(e, jj, slot):
        return [pltpu.make_async_copy(
            w_hbm.at[layer, e, :, pl.ds(pl.multiple_of(c0 + jj * width, width), width)],
            stage.at[slot, n], sem.at[slot, n]) for n, c0 in enumerate(col_starts)]

    @pl.when((j == 0) & (i == 0))
    def _():
        cnt_ref[0] = 0
        for c in copies(be_ref[0], 0, 0):
            c.start()

    @pl.when((i < nu) & _expert_changed(be_ref, i))
    def _():
        k = cnt_ref[0]
        slot = k % 2
        for c in copies(be_ref[i], j, slot):
            c.wait()
        for n, dst in enumerate(dsts):
            dst[...] = stage[slot, n].astype(BF16)
        cnt_ref[0] = k + 1
        jn = jnp.where(be_ref[i] == be_ref[nu - 1], j + 1, j)

        @pl.when(jn < pl.num_programs(0))
        def _():
            for c in copies(nxt_ref[i], jn, 1 - slot):
                c.start()


def _expert_up_kernel(layer, be_ref, nxt_ref, nused_ref, x_ref, bg_ref, bu_ref, w_hbm, h_ref,
                      stage, wg_scr, wu_scr, cnt_ref, sem):
    i = pl.program_id(1)
    active = i < nused_ref[0]
    _stream_weight_tile(be_ref, nxt_ref, nused_ref, w_hbm, layer, (0, D_FF), MOE_TF, stage,
                        (wg_scr, wu_scr), cnt_ref, sem)

    @pl.when(active)
    def _():
        x = x_ref[...]
        gate = jnp.dot(x, wg_scr[...], preferred_element_type=F32) + bg_ref[0, 0]
        up = jnp.dot(x, wu_scr[...], preferred_element_type=F32) + bu_ref[0, 0]
        gate = jnp.minimum(gate, SWIGLU_LIMIT)
        up = jnp.clip(up, -SWIGLU_LIMIT, SWIGLU_LIMIT)
        h_ref[...] = ((up + 1.0) * (gate * jax.nn.sigmoid(SWIGLU_ALPHA * gate))).astype(h_ref.dtype)

    @pl.when(i >= nused_ref[0])
    def _():
        h_ref[...] = jnp.zeros_like(h_ref)


def _expert_up(xg, w_gu, b_gu, blk_e, nxt_e, n_used, layer):
    nf = D_FF // MOE_TF

    def used(i, nu):
        return jnp.minimum(i, nu[0] - 1)

    def bspec(col0):
        return pl.BlockSpec((1, 1, 1, MOE_TF),
                            lambda j, i, be, nx, nu: (layer, be[used(i, nu)], 0, col0 + j))

    return pl.pallas_call(
        functools.partial(_expert_up_kernel, layer),
        grid_spec=pltpu.PrefetchScalarGridSpec(
            num_scalar_prefetch=3,
            grid=(nf, MOE_NB),
            in_specs=[
                pl.BlockSpec((MOE_TB, D_MODEL), lambda j, i, be, nx, nu: (used(i, nu), 0)),
                bspec(0), bspec(nf), pl.BlockSpec(memory_space=pl.ANY),
            ],
            out_specs=pl.BlockSpec((MOE_TB, MOE_TF), lambda j, i, be, nx, nu: (i, j)),
            scratch_shapes=[pltpu.VMEM((2, 2, D_MODEL, MOE_TF), F32),
                            pltpu.VMEM((D_MODEL, MOE_TF), BF16), pltpu.VMEM((D_MODEL, MOE_TF), BF16),
                            pltpu.SMEM((1,), jnp.int32), pltpu.SemaphoreType.DMA((2, 2))],
        ),
        out_shape=jax.ShapeDtypeStruct((MOE_P, D_FF), BF16),
        compiler_params=_cparams(("arbitrary", "arbitrary"), VMEM_LIMIT),
        name="expert_up",
    )(blk_e, nxt_e, n_used, xg, b_gu, b_gu, w_gu)


def _expert_down_kernel(layer, be_ref, nxt_ref, nused_ref, h_ref, b_ref, w_hbm, y_ref,
                        stage, w_scr, cnt_ref, sem):
    i = pl.program_id(1)
    active = i < nused_ref[0]
    _stream_weight_tile(be_ref, nxt_ref, nused_ref, w_hbm, layer, (0,), MOE_TN, stage, (w_scr,),
                        cnt_ref, sem)

    @pl.when(active)
    def _():
        y_ref[...] = jnp.dot(h_ref[...], w_scr[...], preferred_element_type=F32) + b_ref[0, 0]

    @pl.when(i >= nused_ref[0])
    def _():
        y_ref[...] = jnp.zeros_like(y_ref)


def _expert_down(h, w_dn, b_dn, blk_e, nxt_e, n_used, layer):
    nn = D_MODEL // MOE_TN

    def used(i, nu):
        return jnp.minimum(i, nu[0] - 1)

    return pl.pallas_call(
        functools.partial(_expert_down_kernel, layer),
        grid_spec=pltpu.PrefetchScalarGridSpec(
            num_scalar_prefetch=3,
            grid=(nn, MOE_NB),
            in_specs=[
                pl.BlockSpec((MOE_TB, D_FF), lambda j, i, be, nx, nu: (used(i, nu), 0)),
                pl.BlockSpec((1, 1, 1, MOE_TN),
                             lambda j, i, be, nx, nu: (layer, be[used(i, nu)], 0, j)),
                pl.BlockSpec(memory_space=pl.ANY),
            ],
            out_specs=pl.BlockSpec((MOE_TB, MOE_TN), lambda j, i, be, nx, nu: (i, j)),
            scratch_shapes=[pltpu.VMEM((2, 1, D_FF, MOE_TN), F32), pltpu.VMEM((D_FF, MOE_TN), BF16),
                            pltpu.SMEM((1,), jnp.int32), pltpu.SemaphoreType.DMA((2, 1))],
        ),
        out_shape=jax.ShapeDtypeStruct((MOE_P, D_MODEL), F32),
        compiler_params=_cparams(("arbitrary", "arbitrary"), VMEM_LIMIT),
        name="expert_down",
    )(blk_e, nxt_e, n_used, h, b_dn, w_dn)


def _combine_kernel(pos_ref, pos_next_ref, gates_ref, x1_ref, y_hbm, g_ref, b_ref, x2_ref, x2b_ref,
                    buf, sem):
    i = pl.program_id(0)
    slot = i % 2
    n = CMB_T * TOP_K

    @pl.when(i == 0)
    def _():
        _issue_rows(y_hbm, pos_ref, n, buf.at[0], sem.at[0])

    @pl.when(i + 1 < pl.num_programs(0))
    def _():
        _issue_rows(y_hbm, pos_next_ref, n, buf.at[1 - slot], sem.at[1 - slot])

    _wait_rows(y_hbm, n, buf.at[slot], sem.at[slot])
    gates = gates_ref[...]
    y = jnp.zeros((CMB_T, D_MODEL), F32)
    for k in range(TOP_K):
        y = y + buf[slot, k * CMB_T:(k + 1) * CMB_T, :] * gates[:, k:k + 1]
    x2 = _layernorm(DN_ALPHA * x1_ref[...] + y, g_ref[...], b_ref[...])
    x2_ref[...] = x2
    x2b_ref[...] = x2.astype(BF16)


def _moe_combine(pos_kmajor, gates, x1, y, g, b):
    row = pl.BlockSpec((CMB_T, D_MODEL), lambda i: (i, 0))
    vec = pl.BlockSpec((1, D_MODEL), lambda i: (0, 0))
    steps = N_TOK // CMB_T
    return pl.pallas_call(
        _combine_kernel,
        grid=(steps,),
        in_specs=[pl.BlockSpec((CMB_T * TOP_K,), lambda i: (i,), memory_space=pltpu.SMEM),
                  pl.BlockSpec((CMB_T * TOP_K,), lambda i: (jnp.minimum(i + 1, steps - 1),),
                               memory_space=pltpu.SMEM),
                  pl.BlockSpec((CMB_T, TOP_K), lambda i: (i, 0)),
                  row, pl.BlockSpec(memory_space=pl.ANY), vec, vec],
        out_specs=[row, row],
        out_shape=[jax.ShapeDtypeStruct((N_TOK, D_MODEL), F32),
                   jax.ShapeDtypeStruct((N_TOK, D_MODEL), BF16)],
        scratch_shapes=[pltpu.VMEM((2, CMB_T * TOP_K, D_MODEL), F32),
                        pltpu.SemaphoreType.DMA((2,))],
        compiler_params=_cparams(("arbitrary",)),
        name="moe_combine",
    )(pos_kmajor, pos_kmajor, gates, x1, y, g, b)


def _permute_w_in(w):
    parts, off = [], 0
    for n in REF_IN_SIZES:
        parts.append((off, n))
        off += n
    return jnp.concatenate([w[:, parts[i][0]:parts[i][0] + parts[i][1]] for i in MY_ORDER],
                           axis=1).astype(BF16)


def kernel(x_prompt, x_sample, cache_conv, state_ret, cache_swa_k, cache_swa_v, cache_mem_k, cache_mem_v, mem_prompt, w_in, conv_w, ret_gn_w, attn_sinks, w_mem_kv, w_o, ln1_g, ln1_b, router_w, router_b, w_gate_up, b_gate_up, w_down, b_down, ln2_g, ln2_b):
    pos_p = jnp.arange(SEQ, dtype=jnp.int32)
    pos_s = PAST_LEN + jnp.arange(DEC_SEQ, dtype=jnp.int32)
    ret_tabs_p = _ret_tables(pos_p, R_CHUNK)
    ret_tabs_s = _ret_tables(pos_s, DEC_SEQ)
    swa_tabs_p = _swa_tables(pos_p)
    swa_tabs_s = _swa_tables(pos_s)
    kvw = A_KV * A_HD

    xp = x_prompt.reshape(N_PROMPT, D_MODEL)
    xs = x_sample.reshape(N_SAMPLE, D_MODEL)
    xb = jnp.concatenate([xp, xs], axis=0).astype(BF16)
    mem_b = mem_prompt.reshape(BATCH * N_MEM, D_MODEL).astype(BF16)
    zeros_e = jnp.zeros((256, D_MODEL), F32)
    b_gu = b_gate_up.reshape(DEPTH, N_EXP, 1, 2 * D_FF)
    b_dn = b_down.reshape(DEPTH, N_EXP, 1, D_MODEL)
    swa_k = cache_swa_k.reshape(DEPTH, DEC_BATCH, WINDOW, kvw)
    swa_v = cache_swa_v.reshape(DEPTH, DEC_BATCH, WINDOW, kvw)

    outs = {k: [] for k in ("p_conv", "p_ret", "p_k", "p_v", "p_mk", "p_mv",
                            "s_conv", "s_ret", "s_k", "s_v")}
    for l in range(DEPTH):
        z = _matmul(xb, _permute_w_in(w_in[l]), 2304, 512)
        gn = ret_gn_w[l].reshape(1, D_MODEL)
        sinks = attn_sinks[l]

        ya_p, tail_p = _conv_branch(z, conv_w[l], zeros_e, zeros_e, 0, N_PROMPT, SEQ, False)
        yb_p, ret_p = _ret_prompt(z, ret_tabs_p, gn)
        yc_p, kr_p = _swa_prompt(z, sinks, swa_tabs_p)
        mkv = _matmul(mem_b, w_mem_kv[l].astype(BF16), 512, 512)
        mk_p = mkv[:, :D_MODEL].reshape(BATCH, N_MEM, D_MODEL)
        mv_p = mkv[:, D_MODEL:].reshape(BATCH, N_MEM, D_MODEL)
        ym_p = _mem_attend(z, mk_p, mv_p, 0, BATCH, SEQ, 512, "mem_prompt")

        ea = jnp.repeat(cache_conv[l][:, 0], DEC_SEQ, axis=0)
        eb = jnp.repeat(cache_conv[l][:, 1], DEC_SEQ, axis=0)
        ya_s, u_s = _conv_branch(z, conv_w[l], ea, eb, N_PROMPT, N_SAMPLE, DEC_SEQ, True)
        yb_s, ret_s = _ret_sample(z, ret_tabs_s, gn, state_ret, l)
        yc_s, nk_s, nv_s = _swa_sample(z, sinks, swa_tabs_s, swa_k, swa_v, l)
        ym_s = _mem_attend(z, cache_mem_k, cache_mem_v, N_PROMPT, DEC_BATCH, DEC_SEQ, DEC_SEQ,
                           "mem_sample", layer=l)

        wo_b = w_o[l].astype(BF16)
        g1 = ln1_g[l].reshape(1, D_MODEL)
        b1 = ln1_b[l].reshape(1, D_MODEL)
        rw = jnp.pad(router_w[l], ((0, 0), (0, LANES - N_EXP))).astype(BF16)
        rb = jnp.pad(router_b[l], (0, LANES - N_EXP)).reshape(1, LANES)
        x1_p, lg_p = _wo_ln(ya_p, yb_p, yc_p, ym_p, xp, wo_b, g1, b1, rw, rb, "wo_ln_prompt")
        x1_s, lg_s = _wo_ln(ya_s, yb_s, yc_s, ym_s, xs, wo_b, g1, b1, rw, rb, "wo_ln_sample")
        x1 = jnp.concatenate([x1_p, x1_s], axis=0)
        logits = jnp.concatenate([lg_p, lg_s], axis=0)[:, :N_EXP]

        gates, row_tok, pos, blk_e, nxt_e, n_used = _route(logits)
        xg = _moe_gather(x1, row_tok, n_used)
        h = _expert_up(xg, w_gate_up, b_gu, blk_e, nxt_e, n_used, l)
        y = _expert_down(h, w_down, b_dn, blk_e, nxt_e, n_used, l)
        pos_k = pos.reshape(N_TOK // CMB_T, CMB_T, TOP_K).transpose(0, 2, 1).reshape(-1)
        x2, xb = _moe_combine(pos_k, gates, x1, y, ln2_g[l].reshape(1, D_MODEL),
                              ln2_b[l].reshape(1, D_MODEL))
        xp, xs = x2[:N_PROMPT], x2[N_PROMPT:]

        tail_p = tail_p.reshape(BATCH, SEQ // 256, SUBLANES, D_MODEL)
        outs["p_conv"].append(tail_p[:, -1, -(CONV_W - 1):])
        outs["p_ret"].append(ret_p)
        outs["p_k"].append(kr_p.reshape(BATCH, SEQ, A_KV, A_HD)[:, -WINDOW:])
        outs["p_v"].append(z[:N_PROMPT, AV:AV + kvw].reshape(BATCH, SEQ, A_KV, A_HD)[:, -WINDOW:])
        outs["p_mk"].append(mk_p.reshape(BATCH, N_MEM, M_HEADS, M_HD))
        outs["p_mv"].append(mv_p.reshape(BATCH, N_MEM, M_HEADS, M_HD))
        outs["s_conv"].append(u_s.reshape(DEC_BATCH, DEC_SEQ, D_MODEL)[:, -(CONV_W - 1):])
        outs["s_ret"].append(ret_s)
        outs["s_k"].append(nk_s.reshape(DEC_BATCH, WINDOW, A_KV, A_HD))
        outs["s_v"].append(nv_s.reshape(DEC_BATCH, WINDOW, A_KV, A_HD))

    return (xp.reshape(BATCH, SEQ, D_MODEL), xs.reshape(DEC_BATCH, DEC_SEQ, D_MODEL),
            jnp.stack(outs["p_conv"]), jnp.stack(outs["p_ret"]), jnp.stack(outs["p_k"]),
            jnp.stack(outs["p_v"]), jnp.stack(outs["p_mk"]), jnp.stack(outs["p_mv"]),
            jnp.stack(outs["s_conv"]), jnp.stack(outs["s_ret"]), jnp.stack(outs["s_k"]),
            jnp.stack(outs["s_v"]))
```

```python
import functools

import jax
import jax.numpy as jnp
from jax import lax
from jax.experimental import pallas as pl
from jax.experimental.pallas import tpu as pltpu

F32 = jnp.float32
BF16 = jnp.bfloat16

D_MODEL = 2048
BATCH = 2
SEQ = 4096
DEPTH = 2
DEC_BATCH = 128
DEC_SEQ = 8
PAST_LEN = 8192
N_PROMPT = BATCH * SEQ
N_SAMPLE = DEC_BATCH * DEC_SEQ
N_TOK = N_PROMPT + N_SAMPLE

CONV_W = 3
R_HEADS = 8
R_DK = 128
R_DV = D_MODEL // R_HEADS
R_CHUNK = 128
R_THETA = 10000.0
A_HEADS = 32
A_KV = 4
A_HD = D_MODEL // A_HEADS
A_GROUP = A_HEADS // A_KV
WINDOW = 128
ROPE_DIMS = A_HD // 4
ROPE_THETA = 500000.0
N_MEM = 256
M_HEADS = 4
M_HD = D_MODEL // M_HEADS
N_EXP = 32
TOP_K = 4
D_FF = D_MODEL
SWIGLU_LIMIT = 7.0
SWIGLU_ALPHA = 1.702
LN_EPS = 1e-5
GN_EPS = 1e-6
DN_ALPHA = (2 * DEPTH) ** 0.25

REF_IN_SIZES = (D_MODEL, D_MODEL, D_MODEL, R_HEADS * R_DK, R_HEADS * R_DK, D_MODEL, D_MODEL,
                D_MODEL, A_KV * A_HD, A_KV * A_HD, D_MODEL, 4 * D_MODEL)
D_IN = sum(REF_IN_SIZES)
MY_ORDER = (11, 0, 1, 2, 5, 6, 7, 10, 3, 4, 8, 9)
GL, HC, BC, CC, RV, RG, AQ, MQ, RQ, RK, AK, AV = (
    0, 8192, 10240, 12288, 14336, 16384, 18432, 20480, 22528, 23552, 24576, 24832)

LANES = 128
SUBLANES = 8
VMEM_LIMIT = 56 * 1024 * 1024

MOE_TB = 256
MOE_NB = N_TOK * TOP_K // MOE_TB + N_EXP
MOE_P = MOE_NB * MOE_TB
MOE_TF = 1024
MOE_TN = 2048
CMB_T = 64


def _cparams(sem, vmem=None):
    return pltpu.CompilerParams(dimension_semantics=sem, vmem_limit_bytes=vmem)


def _mm_kernel(x_ref, w_ref, o_ref):
    o_ref[...] = jnp.dot(x_ref[...], w_ref[...], preferred_element_type=F32).astype(o_ref.dtype)


def _matmul(x, w, tm, tn, out_dtype=F32):
    m, k = x.shape
    n = w.shape[1]
    return pl.pallas_call(
        _mm_kernel,
        grid=(m // tm, n // tn),
        in_specs=[pl.BlockSpec((tm, k), lambda i, j: (i, 0)),
                  pl.BlockSpec((k, tn), lambda i, j: (0, j))],
        out_specs=pl.BlockSpec((tm, tn), lambda i, j: (i, j)),
        out_shape=jax.ShapeDtypeStruct((m, n), out_dtype),
        compiler_params=_cparams(("parallel", "parallel"), VMEM_LIMIT),
        name="matmul",
    )(x, w)


def _conv_kernel(seq_len, full_u, hc_ref, cc_ref, bc_ref, g_ref, hh_ref, ch_ref, ea_ref, eb_ref,
                 w_ref, ya_ref, tail_ref):
    i = pl.program_id(0)
    tm = hc_ref.shape[0]
    u = cc_ref[...] * hc_ref[...]
    ue = jnp.concatenate([ch_ref[...] * hh_ref[...], u], axis=0)
    p1 = pltpu.roll(ue, 1, 0)[SUBLANES:]
    p2 = pltpu.roll(ue, 2, 0)[SUBLANES:]
    t = (i * tm + lax.broadcasted_iota(jnp.int32, (tm, 1), 0)) % seq_len
    ea = ea_ref[...]
    eb = eb_ref[...]
    p1 = jnp.where(t == 0, eb, p1)
    p2 = jnp.where(t == 0, ea, jnp.where(t == 1, eb, p2))
    w = w_ref[...]
    y = p2 * w[0:1] + p1 * w[1:2] + u * w[2:3]
    ya_ref[...] = jax.nn.sigmoid(g_ref[...]) * (bc_ref[...] * y)
    tail_ref[...] = u if full_u else u[tm - SUBLANES:]


def _conv_branch(z, conv_w, ea, eb, row0, rows, seq_len, sample):
    tm, tc = 256, 512
    r0 = row0 // tm
    nj = D_MODEL // tc

    def zspec(col):
        return pl.BlockSpec((tm, tc), lambda i, j: (r0 + i, col // tc + j))

    def halo(col):
        return pl.BlockSpec(
            (SUBLANES, tc),
            lambda i, j: (jnp.maximum((r0 + i) * (tm // SUBLANES) - 1, 0), col // tc + j))

    if sample:
        espec = pl.BlockSpec((tm, tc), lambda i, j: (i, j))
        tail_spec = pl.BlockSpec((tm, tc), lambda i, j: (i, j))
        tail_shape = jax.ShapeDtypeStruct((rows, D_MODEL), F32)
    else:
        espec = pl.BlockSpec((tm, tc), lambda i, j: (0, j))
        tail_spec = pl.BlockSpec((SUBLANES, tc), lambda i, j: (i, j))
        tail_shape = jax.ShapeDtypeStruct((rows // tm * SUBLANES, D_MODEL), F32)
    return pl.pallas_call(
        functools.partial(_conv_kernel, seq_len, sample),
        grid=(rows // tm, nj),
        in_specs=[zspec(HC), zspec(CC), zspec(BC), zspec(GL), halo(HC), halo(CC), espec, espec,
                  pl.BlockSpec((CONV_W, tc), lambda i, j: (0, j))],
        out_specs=[pl.BlockSpec((tm, tc), lambda i, j: (i, j)), tail_spec],
        out_shape=[jax.ShapeDtypeStruct((rows, D_MODEL), F32), tail_shape],
        compiler_params=_cparams(("parallel", "parallel"), VMEM_LIMIT),
        name="conv_sample" if sample else "conv_prompt",
    )(z, z, z, z, z, z, ea, eb, conv_w)


def _rope_full(x, cos, sin):
    return x * cos + pltpu.roll(x, R_DK // 2, 1) * sin


def _bf16_exact(x):
    return x.astype(BF16).astype(F32)


def _group_norm_gate(o, gn, rg, g):
    mu = jnp.mean(o, axis=-1, keepdims=True)
    d = o - mu
    var = jnp.mean(d * d, axis=-1, keepdims=True)
    yn = d * lax.rsqrt(var + GN_EPS) * gn
    return jax.nn.sigmoid(g) * ((rg * jax.nn.sigmoid(rg)) * yn)


def _ret_prompt_kernel(q_ref, k_ref, v_ref, rg_ref, g_ref, cos_ref, sin_ref, dmask_ref, qdec_ref,
                       kdec_ref, cdec_ref, gn_ref, y_ref, sfin_ref, s_scr):
    c = pl.program_id(1)

    @pl.when(c == 0)
    def _():
        s_scr[...] = jnp.zeros_like(s_scr)

    cos = cos_ref[...]
    sin = sin_ref[...]
    for h in range(R_HEADS):
        qs = slice(h * R_DK, (h + 1) * R_DK)
        vs = slice(h * R_DV, (h + 1) * R_DV)
        q = _rope_full(q_ref[:, qs], cos, sin)
        k = _rope_full(k_ref[:, qs], cos, sin) * (R_DK ** -0.5)
        vb = v_ref[:, vs].astype(BF16)
        s = s_scr[h]
        att = lax.dot_general(q.astype(BF16), k.astype(BF16), (((1,), (1,)), ((), ())),
                              preferred_element_type=F32) * dmask_ref[h]
        o = (jnp.dot(att.astype(BF16), vb, preferred_element_type=F32)
             + jnp.dot((q * qdec_ref[h]).astype(BF16), s.astype(BF16), preferred_element_type=F32))
        kd = (k * kdec_ref[h]).astype(BF16)
        s_new = cdec_ref[h] * s + lax.dot_general(kd, vb, (((0,), (0,)), ((), ())),
                                                  preferred_element_type=F32)
        s_scr[h] = s_new
        sfin_ref[0, h] = s_new
        y_ref[:, vs] = _group_norm_gate(o, gn_ref[:, vs], rg_ref[:, vs], g_ref[:, vs])


def _ret_prompt(z, tabs, gn_w):
    cos, sin, dmask, qdec, kdec, cdec = tabs
    nc = SEQ // R_CHUNK

    def zspec(col, width):
        return pl.BlockSpec((R_CHUNK, width), lambda b, c: (b * nc + c, col // width))

    def full(a):
        return pl.BlockSpec(a.shape, lambda b, c: (0,) * a.ndim)

    tspec = pl.BlockSpec((R_CHUNK, R_DK), lambda b, c: (c, 0))
    return pl.pallas_call(
        _ret_prompt_kernel,
        grid=(BATCH, nc),
        in_specs=[zspec(RQ, R_HEADS * R_DK), zspec(RK, R_HEADS * R_DK), zspec(RV, D_MODEL),
                  zspec(RG, D_MODEL), zspec(GL + D_MODEL, D_MODEL), tspec, tspec,
                  full(dmask), full(qdec), full(kdec), full(cdec), full(gn_w)],
        out_specs=[pl.BlockSpec((R_CHUNK, D_MODEL), lambda b, c: (b * nc + c, 0)),
                   pl.BlockSpec((1, R_HEADS, R_DK, R_DV), lambda b, c: (b, 0, 0, 0))],
        out_shape=[jax.ShapeDtypeStruct((N_PROMPT, D_MODEL), F32),
                   jax.ShapeDtypeStruct((BATCH, R_HEADS, R_DK, R_DV), F32)],
        scratch_shapes=[pltpu.VMEM((R_HEADS, R_DK, R_DV), F32)],
        compiler_params=_cparams(("parallel", "arbitrary"), VMEM_LIMIT),
        name="retention_prompt",
    )(z, z, z, z, z, cos, sin, dmask, qdec, kdec, cdec, gn_w)


def _ret_sample_kernel(q_ref, k_ref, v_ref, rg_ref, g_ref, cos_ref, sin_ref, dmask_ref, qdec_ref,
                       kdec_ref, cdec_ref, gn_ref, s_ref, y_ref, snew_ref):
    cos = cos_ref[...]
    sin = sin_ref[...]
    for h in range(R_HEADS):
        qs = slice(h * R_DK, (h + 1) * R_DK)
        vs = slice(h * R_DV, (h + 1) * R_DV)
        q = _rope_full(q_ref[:, qs], cos, sin)
        k = _rope_full(k_ref[:, qs], cos, sin) * (R_DK ** -0.5)
        v = _bf16_exact(v_ref[:, vs])
        s = s_ref[0, 0, h]
        att = lax.dot_general(_bf16_exact(q), _bf16_exact(k), (((1,), (1,)), ((), ())),
                              preferred_element_type=F32) * dmask_ref[h]
        o = (jnp.dot(_bf16_exact(att), v, preferred_element_type=F32)
             + jnp.dot((q * qdec_ref[h]).astype(BF16), s.astype(BF16), preferred_element_type=F32))
        kd = _bf16_exact(k * kdec_ref[h])
        snew_ref[0, h] = cdec_ref[h] * s + lax.dot_general(
            kd, v, (((0,), (0,)), ((), ())), preferred_element_type=F32)
        y_ref[:, vs] = _group_norm_gate(o, gn_ref[:, vs], rg_ref[:, vs], g_ref[:, vs])


def _ret_sample(z, tabs, gn_w, state, layer):
    cos, sin, dmask, qdec, kdec, cdec = tabs
    r0 = N_PROMPT // DEC_SEQ

    def zspec(col, width):
        return pl.BlockSpec((DEC_SEQ, width), lambda b: (r0 + b, col // width))

    def full(a):
        return pl.BlockSpec(a.shape, lambda b: (0,) * a.ndim)

    sspec = pl.BlockSpec((1, R_HEADS, R_DK, R_DV), lambda b: (b, 0, 0, 0))
    sin_spec = pl.BlockSpec((1, 1, R_HEADS, R_DK, R_DV), lambda b: (layer, b, 0, 0, 0))
    return pl.pallas_call(
        _ret_sample_kernel,
        grid=(DEC_BATCH,),
        in_specs=[zspec(RQ, R_HEADS * R_DK), zspec(RK, R_HEADS * R_DK), zspec(RV, D_MODEL),
                  zspec(RG, D_MODEL), zspec(GL + D_MODEL, D_MODEL),
                  full(cos), full(sin), full(dmask), full(qdec), full(kdec), full(cdec), full(gn_w),
                  sin_spec],
        out_specs=[pl.BlockSpec((DEC_SEQ, D_MODEL), lambda b: (b, 0)), sspec],
        out_shape=[jax.ShapeDtypeStruct((N_SAMPLE, D_MODEL), F32),
                   jax.ShapeDtypeStruct(state.shape[1:], F32)],
        compiler_params=_cparams(("parallel",)),
        name="retention_sample",
    )(z, z, z, z, z, cos, sin, dmask, qdec, kdec, cdec, gn_w, state)


def _ret_tables(pos, chunk):
    half = R_DK // 2
    inv = R_THETA ** (-jnp.arange(half, dtype=F32) / half)
    ang = pos.astype(F32)[:, None] * inv[None, :]
    cos = jnp.concatenate([jnp.cos(ang), jnp.cos(ang)], axis=-1)
    sin = jnp.concatenate([-jnp.sin(ang), jnp.sin(ang)], axis=-1)
    lg = jnp.log(1.0 - 2.0 ** (-5.0 - jnp.arange(R_HEADS, dtype=F32)))
    i = jnp.arange(chunk, dtype=F32)
    diff = i[:, None] - i[None, :]
    dmask = jnp.where(diff >= 0, jnp.exp(lg[:, None, None] * jnp.maximum(diff, 0.0)), 0.0)
    qdec = jnp.broadcast_to(jnp.exp(lg[:, None] * (i + 1.0))[:, :, None], (R_HEADS, chunk, R_DK))
    kdec = jnp.broadcast_to(jnp.exp(lg[:, None] * (chunk - 1.0 - i))[:, :, None],
                            (R_HEADS, chunk, R_DK))
    cdec = jnp.broadcast_to(jnp.exp(lg * chunk)[:, None, None], (R_HEADS, 1, R_DV))
    return cos, sin, dmask.astype(F32), qdec.astype(F32), kdec.astype(F32), cdec.astype(F32)


def _swa_tables(pos):
    half = ROPE_DIMS // 2
    inv = ROPE_THETA ** (-jnp.arange(half, dtype=F32) / half)
    ang = pos.astype(F32)[:, None] * inv[None, :]
    cos, sin = jnp.cos(ang), jnp.sin(ang)
    t = pos.shape[0]
    ones = jnp.ones((t, A_HD - ROPE_DIMS), F32)
    zeros = jnp.zeros((t, A_HD - ROPE_DIMS), F32)
    zh = jnp.zeros((t, half), F32)
    c = jnp.concatenate([cos, cos, ones], axis=-1)
    s_lo = jnp.concatenate([-sin, zh, zeros], axis=-1)
    s_hi = jnp.concatenate([zh, sin, zeros], axis=-1)
    rep = LANES // A_HD
    return jnp.tile(c, (1, rep)), jnp.tile(s_lo, (1, rep)), jnp.tile(s_hi, (1, rep))


def _rope_partial(x, c, s_lo, s_hi):
    half = ROPE_DIMS // 2
    return x * c + pltpu.roll(x, LANES - half, 1) * s_lo + pltpu.roll(x, half, 1) * s_hi


def _dup_head(chunk, odd):
    lane = lax.broadcasted_iota(jnp.int32, chunk.shape, 1)
    sw = pltpu.roll(chunk, A_HD, 1)
    if odd:
        return jnp.where(lane < A_HD, sw, chunk)
    return jnp.where(lane < A_HD, chunk, sw)


def _swa_core(sink_ref, q_ref, g_ref, y_ref, kp, kc, vp, vc, tabs_q, has_prev, tq, chunks_per_dot):
    cq, slq, shq = tabs_q
    lane_q = lax.broadcasted_iota(jnp.int32, (tq, LANES), 1)
    n_chunks = A_GROUP * A_HD // LANES
    for h in range(A_KV):
        kchunk = slice((h // 2) * LANES, (h // 2 + 1) * LANES)
        kkp = _dup_head(kp[:, kchunk], h % 2).astype(BF16)
        kkc = _dup_head(kc[:, kchunk], h % 2).astype(BF16)
        vvp = _dup_head(vp[:, kchunk], h % 2).astype(BF16)
        vvc = _dup_head(vc[:, kchunk], h % 2).astype(BF16)
        for c0 in range(0, n_chunks, chunks_per_dot):
            pieces, sinks = [], []
            for ci in range(c0, c0 + chunks_per_dot):
                col = h * A_GROUP * A_HD + ci * LANES
                qc = _rope_partial(q_ref[:, col:col + LANES], cq, slq, shq) * (A_HD ** -0.5)
                for half in range(2):
                    keep = (lane_q < A_HD) if half == 0 else (lane_q >= A_HD)
                    pieces.append(jnp.where(keep, qc, 0.0).astype(BF16))
                    head = h * A_GROUP + 2 * ci + half
                    sinks.append(jnp.full((tq, 1), sink_ref[head], F32))
            qs = jnp.concatenate(pieces, axis=0)
            sink = jnp.concatenate(sinks, axis=0)
            r = qs.shape[0]
            nt = (((1,), (1,)), ((), ()))
            sp = lax.dot_general(qs, kkp, nt, preferred_element_type=F32)
            sc = lax.dot_general(qs, kkc, nt, preferred_element_type=F32)
            t = lax.broadcasted_iota(jnp.int32, (r, WINDOW), 0) % tq
            j = lax.broadcasted_iota(jnp.int32, (r, WINDOW), 1)
            sp = jnp.where((j >= t) & has_prev, sp, -jnp.inf)
            sc = jnp.where(j <= t, sc, -jnp.inf)
            m = jnp.maximum(jnp.max(jnp.maximum(sp, sc), axis=-1, keepdims=True), sink)
            pp = jnp.exp(sp - m)
            pc = jnp.exp(sc - m)
            den = jnp.sum(pp + pc, axis=-1, keepdims=True) + jnp.exp(sink - m)
            inv = 1.0 / den
            o = (jnp.dot((pp * inv).astype(BF16), vvp, preferred_element_type=F32)
                 + jnp.dot((pc * inv).astype(BF16), vvc, preferred_element_type=F32))
            for n, ci in enumerate(range(c0, c0 + chunks_per_dot)):
                col = h * A_GROUP * A_HD + ci * LANES
                oa = o[(2 * n) * tq:(2 * n + 1) * tq]
                ob = o[(2 * n + 1) * tq:(2 * n + 2) * tq]
                oc = jnp.where(lane_q < A_HD, oa, ob)
                y_ref[:, col:col + LANES] = jax.nn.sigmoid(g_ref[:, col:col + LANES]) * oc


def _rope_kv(k_ref, ck, slk, shk):
    parts = [_rope_partial(k_ref[:, c * LANES:(c + 1) * LANES], ck, slk, shk)
             for c in range(A_KV * A_HD // LANES)]
    return jnp.concatenate(parts, axis=-1)


def _swa_prompt_kernel(sink_ref, q_ref, k_ref, v_ref, g_ref, c_ref, sl_ref, sh_ref,
                       y_ref, kr_ref, kp_scr, vp_scr):
    n = pl.program_id(1)

    @pl.when(n == 0)
    def _():
        kp_scr[...] = jnp.zeros_like(kp_scr)
        vp_scr[...] = jnp.zeros_like(vp_scr)

    tabs = (c_ref[...], sl_ref[...], sh_ref[...])
    kc = _rope_kv(k_ref, *tabs)
    vc = v_ref[...]
    kr_ref[...] = kc
    _swa_core(sink_ref, q_ref, g_ref, y_ref, kp_scr[...], kc, vp_scr[...], vc, tabs, n > 0,
              WINDOW, 1)
    kp_scr[...] = kc
    vp_scr[...] = vc


def _swa_prompt(z, sinks, tabs):
    nb = SEQ // WINDOW
    kvw = A_KV * A_HD

    def zspec(col, width):
        return pl.BlockSpec((WINDOW, width), lambda b, n: (b * nb + n, col // width))

    tspec = pl.BlockSpec((WINDOW, LANES), lambda b, n: (n, 0))
    return pl.pallas_call(
        _swa_prompt_kernel,
        grid=(BATCH, nb),
        in_specs=[pl.BlockSpec(memory_space=pltpu.SMEM),
                  zspec(AQ, D_MODEL), zspec(AK, kvw), zspec(AV, kvw), zspec(GL + 2 * D_MODEL, D_MODEL),
                  tspec, tspec, tspec],
        out_specs=[pl.BlockSpec((WINDOW, D_MODEL), lambda b, n: (b * nb + n, 0)),
                   pl.BlockSpec((WINDOW, kvw), lambda b, n: (b * nb + n, 0))],
        out_shape=[jax.ShapeDtypeStruct((N_PROMPT, D_MODEL), F32),
                   jax.ShapeDtypeStruct((N_PROMPT, kvw), F32)],
        scratch_shapes=[pltpu.VMEM((WINDOW, kvw), F32), pltpu.VMEM((WINDOW, kvw), F32)],
        compiler_params=_cparams(("parallel", "arbitrary"), VMEM_LIMIT),
        name="swa_prompt",
    )(sinks, z, z, z, z, *tabs)


def _swa_sample_kernel(sink_ref, q_ref, k_ref, v_ref, g_ref, c_ref, sl_ref, sh_ref, ck_ref, cv_ref,
                       y_ref, nk_ref, nv_ref):
    tabs = (c_ref[...], sl_ref[...], sh_ref[...])
    kn = _rope_kv(k_ref, *tabs)
    vn = v_ref[...]
    pad = jnp.zeros((WINDOW - DEC_SEQ, A_KV * A_HD), F32)
    kp = ck_ref[0, 0]
    vp = cv_ref[0, 0]
    _swa_core(sink_ref, q_ref, g_ref, y_ref, kp, jnp.concatenate([kn, pad], axis=0),
              vp, jnp.concatenate([vn, pad], axis=0), tabs, True, DEC_SEQ,
              A_GROUP * A_HD // LANES)
    nk_ref[0] = jnp.concatenate([kp[DEC_SEQ:], kn], axis=0)
    nv_ref[0] = jnp.concatenate([vp[DEC_SEQ:], vn], axis=0)


def _swa_sample(z, sinks, tabs, cache_k, cache_v, layer):
    kvw = A_KV * A_HD
    r0 = N_PROMPT // DEC_SEQ

    def zspec(col, width):
        return pl.BlockSpec((DEC_SEQ, width), lambda b: (r0 + b, col // width))

    tspec = pl.BlockSpec((DEC_SEQ, LANES), lambda b: (0, 0))
    cspec = pl.BlockSpec((1, WINDOW, kvw), lambda b: (b, 0, 0))
    cin_spec = pl.BlockSpec((1, 1, WINDOW, kvw), lambda b: (layer, b, 0, 0))
    return pl.pallas_call(
        _swa_sample_kernel,
        grid=(DEC_BATCH,),
        in_specs=[pl.BlockSpec(memory_space=pltpu.SMEM),
                  zspec(AQ, D_MODEL), zspec(AK, kvw), zspec(AV, kvw), zspec(GL + 2 * D_MODEL, D_MODEL),
                  tspec, tspec, tspec, cin_spec, cin_spec],
        out_specs=[pl.BlockSpec((DEC_SEQ, D_MODEL), lambda b: (b, 0)), cspec, cspec],
        out_shape=[jax.ShapeDtypeStruct((N_SAMPLE, D_MODEL), F32),
                   jax.ShapeDtypeStruct((DEC_BATCH, WINDOW, kvw), F32),
                   jax.ShapeDtypeStruct((DEC_BATCH, WINDOW, kvw), F32)],
        compiler_params=_cparams(("parallel",)),
        name="swa_sample",
    )(sinks, z, z, z, z, *tabs, cache_k, cache_v)


def _mem_kernel(q_ref, k_ref, v_ref, g_ref, y_ref):
    for h in range(M_HEADS):
        hs = slice(h * M_HD, (h + 1) * M_HD)
        q = q_ref[:, hs].astype(BF16)
        k = k_ref[0, :, hs].astype(BF16)
        v = v_ref[0, :, hs].astype(BF16)
        s = lax.dot_general(q, k, (((1,), (1,)), ((), ())), preferred_element_type=F32) * (M_HD ** -0.5)
        m = jnp.max(s, axis=-1, keepdims=True)
        p = jnp.exp(s - m)
        p = p / jnp.sum(p, axis=-1, keepdims=True)
        o = jnp.dot(p.astype(BF16), v, preferred_element_type=F32)
        y_ref[:, hs] = jax.nn.sigmoid(g_ref[:, hs]) * o


def _mem_cache_kernel(q_ref, k_ref, v_ref, g_ref, y_ref):
    k2 = k_ref.reshape(N_MEM * M_HEADS, M_HD)
    v2 = v_ref.reshape(N_MEM * M_HEADS, M_HD)
    tq = q_ref.shape[0]
    qs = jnp.concatenate([q_ref[:, h * M_HD:(h + 1) * M_HD] for h in range(M_HEADS)],
                         axis=0).astype(BF16)
    s = lax.dot_general(qs, k2[...].astype(BF16), (((1,), (1,)), ((), ())),
                        preferred_element_type=F32) * (M_HD ** -0.5)
    row_h = lax.broadcasted_iota(jnp.int32, s.shape, 0) // tq
    col_h = lax.broadcasted_iota(jnp.int32, s.shape, 1) % M_HEADS
    s = jnp.where(row_h == col_h, s, -jnp.inf)
    m = jnp.max(s, axis=-1, keepdims=True)
    p = jnp.exp(s - m)
    p = p / jnp.sum(p, axis=-1, keepdims=True)
    o = jnp.dot(p.astype(BF16), v2[...].astype(BF16), preferred_element_type=F32)
    for h in range(M_HEADS):
        hs = slice(h * M_HD, (h + 1) * M_HD)
        y_ref[:, hs] = jax.nn.sigmoid(g_ref[:, hs]) * o[h * tq:(h + 1) * tq]


def _mem_attend(z, mk, mv, row0, nb, rows_per_b, tq, name, layer=None):
    nq = rows_per_b // tq
    r0 = row0 // tq

    def zspec(col):
        return pl.BlockSpec((tq, D_MODEL), lambda b, i: (r0 + b * nq + i, col // D_MODEL))

    if layer is None:
        kvspec = pl.BlockSpec((1, N_MEM, D_MODEL), lambda b, i: (b, 0, 0))
    else:
        kvspec = pl.BlockSpec((None, None, N_MEM, M_HEADS, M_HD),
                              lambda b, i: (layer, b, 0, 0, 0))
    return pl.pallas_call(
        _mem_kernel if layer is None else _mem_cache_kernel,
        grid=(nb, nq),
        in_specs=[zspec(MQ), kvspec, kvspec, zspec(GL + 3 * D_MODEL)],
        out_specs=pl.BlockSpec((tq, D_MODEL), lambda b, i: (b * nq + i, 0)),
        out_shape=jax.ShapeDtypeStruct((nb * rows_per_b, D_MODEL), F32),
        compiler_params=_cparams(("parallel", "parallel"), VMEM_LIMIT),
        name=name,
    )(z, mk, mv, z)


def _layernorm(r, g, b):
    mu = jnp.mean(r, axis=-1, keepdims=True)
    d = r - mu
    var = jnp.mean(d * d, axis=-1, keepdims=True)
    return d * lax.rsqrt(var + LN_EPS) * g + b


def _wo_ln_kernel(ya_ref, yb_ref, yc_ref, ym_ref, x_ref, wo_ref, g_ref, b_ref, rw_ref, rb_ref,
                  x1_ref, lg_ref):
    merged = ya_ref[...] + yb_ref[...] + yc_ref[...] + ym_ref[...]
    r = DN_ALPHA * x_ref[...] + jnp.dot(merged.astype(BF16), wo_ref[...], preferred_element_type=F32)
    x1 = _layernorm(r, g_ref[...], b_ref[...])
    x1_ref[...] = x1
    lg_ref[...] = jnp.dot(x1.astype(BF16), rw_ref[...], preferred_element_type=F32) + rb_ref[...]


def _wo_ln_both_kernel(n_prompt_tiles, *refs):
    i = pl.program_id(0)
    prompt, sample, rest = refs[:5], refs[5:10], refs[10:]

    @pl.when(i < n_prompt_tiles)
    def _():
        _wo_ln_kernel(*prompt, *rest)

    @pl.when(i >= n_prompt_tiles)
    def _():
        _wo_ln_kernel(*sample, *rest)


def _wo_ln(branches_p, x_p, branches_s, x_s, xs_row0, wo, g, b, rw, rb):
    tm = 128
    npt = N_PROMPT // tm
    xs0 = xs_row0 // tm
    p_row = pl.BlockSpec((tm, D_MODEL), lambda i: (jnp.minimum(i, npt - 1), 0))
    s_row = pl.BlockSpec((tm, D_MODEL), lambda i: (jnp.maximum(i - npt, 0), 0))
    sx_row = pl.BlockSpec((tm, D_MODEL), lambda i: (xs0 + jnp.maximum(i - npt, 0), 0))

    def full(a):
        return pl.BlockSpec(a.shape, lambda i: (0,) * a.ndim)

    return pl.pallas_call(
        functools.partial(_wo_ln_both_kernel, npt),
        grid=(N_TOK // tm,),
        in_specs=[p_row] * 5 + [s_row] * 4 + [sx_row]
        + [full(wo), full(g), full(b), full(rw), full(rb)],
        out_specs=[pl.BlockSpec((tm, D_MODEL), lambda i: (i, 0)),
                   pl.BlockSpec((tm, LANES), lambda i: (i, 0))],
        out_shape=[jax.ShapeDtypeStruct((N_TOK, D_MODEL), F32),
                   jax.ShapeDtypeStruct((N_TOK, LANES), F32)],
        compiler_params=_cparams(("arbitrary",), VMEM_LIMIT),
        name="wo_ln",
    )(*branches_p, x_p, *branches_s, x_s, wo, g, b, rw, rb)


def _route(logits):
    top_v, top_i = lax.top_k(logits, TOP_K)
    gates = jax.nn.softmax(top_v, axis=-1)
    na = N_TOK * TOP_K
    flat_e = top_i.reshape(-1).astype(jnp.int32)
    order = jnp.argsort(flat_e).astype(jnp.int32)
    inv_order = jnp.argsort(order).astype(jnp.int32)
    se = flat_e[order]
    eids = jnp.arange(N_EXP, dtype=jnp.int32)
    counts = jnp.sum((flat_e[:, None] == eids[None, :]).astype(jnp.int32), axis=0)
    padded = (counts + MOE_TB - 1) // MOE_TB * MOE_TB
    pend = jnp.cumsum(padded)
    pstart = pend - padded
    ustart = jnp.cumsum(counts) - counts
    dest = pstart[se] + jnp.arange(na, dtype=jnp.int32) - ustart[se]
    pos = dest[inv_order]
    blk_start = jnp.arange(MOE_NB, dtype=jnp.int32) * MOE_TB
    blk_e = jnp.minimum(jnp.sum((pend[None, :] <= blk_start[:, None]).astype(jnp.int32), axis=1),
                        N_EXP - 1)
    off = (blk_start - pstart[blk_e])[:, None] + jnp.arange(MOE_TB, dtype=jnp.int32)[None, :]
    src = jnp.clip(ustart[blk_e][:, None] + off, 0, na - 1)
    row_tok = jnp.where(off < counts[blk_e][:, None], order[src] // TOP_K, 0).reshape(-1)
    n_used = (pend[-1] // MOE_TB).astype(jnp.int32).reshape(1)
    has = counts > 0
    later = jnp.where((eids[None, :] > eids[:, None]) & has[None, :], eids[None, :], N_EXP)
    first_e = jnp.min(jnp.where(has, eids, N_EXP))
    nxt_of = jnp.min(later, axis=1)
    nxt_e = jnp.where(nxt_of < N_EXP, nxt_of, first_e)[blk_e].astype(jnp.int32)
    return gates, row_tok, pos, blk_e, nxt_e, n_used


def _row_copy(src_hbm, idx, dst, slot, sem):
    return pltpu.make_async_copy(src_hbm.at[pl.ds(idx, 1)], dst.at[pl.ds(slot, 1)], sem)


def _issue_rows(src_hbm, idx_ref, n, dst, sem):
    def body(r, carry):
        _row_copy(src_hbm, idx_ref[r], dst, r, sem).start()
        return carry

    lax.fori_loop(0, n, body, 0, unroll=8)


def _wait_rows(src_hbm, n, dst, sem):
    pltpu.make_async_copy(src_hbm.at[pl.ds(0, n)], dst, sem).wait()


def _gather_kernel(nused_ref, tok_ref, tok_next_ref, x_hbm, o_ref, buf, sem):
    i = pl.program_id(0)
    nu = nused_ref[0]
    slot = i % 2

    @pl.when(i == 0)
    def _():
        _issue_rows(x_hbm, tok_ref, MOE_TB, buf.at[0], sem.at[0])

    @pl.when(i + 1 < nu)
    def _():
        _issue_rows(x_hbm, tok_next_ref, MOE_TB, buf.at[1 - slot], sem.at[1 - slot])

    @pl.when(i < nu)
    def _():
        _wait_rows(x_hbm, MOE_TB, buf.at[slot], sem.at[slot])
        o_ref[...] = buf[slot].astype(BF16)

    @pl.when(i >= nu)
    def _():
        o_ref[...] = jnp.zeros_like(o_ref)


def _moe_gather(x1, row_tok, n_used):
    return pl.pallas_call(
        _gather_kernel,
        grid_spec=pltpu.PrefetchScalarGridSpec(
            num_scalar_prefetch=1,
            grid=(MOE_NB,),
            in_specs=[pl.BlockSpec((MOE_TB,), lambda i, nu: (i,), memory_space=pltpu.SMEM),
                      pl.BlockSpec((MOE_TB,), lambda i, nu: (jnp.minimum(i + 1, MOE_NB - 1),),
                                   memory_space=pltpu.SMEM),
                      pl.BlockSpec(memory_space=pl.ANY)],
            out_specs=pl.BlockSpec((MOE_TB, D_MODEL), lambda i, nu: (i, 0)),
            scratch_shapes=[pltpu.VMEM((2, MOE_TB, D_MODEL), F32), pltpu.SemaphoreType.DMA((2,))],
        ),
        out_shape=jax.ShapeDtypeStruct((MOE_P, D_MODEL), BF16),
        compiler_params=_cparams(("arbitrary",)),
        name="moe_gather",
    )(n_used, row_tok, row_tok, x1)


def _expert_changed(be_ref, i):
    return (i == 0) | (be_ref[i] != be_ref[jnp.maximum(i - 1, 0)])


def _stream_weight_tile(be_ref, nxt_ref, nused_ref, w_hbm, layer, col_starts, width, stage, dsts,
                        cnt_ref, sem):
    j = pl.program_id(0)
    i = pl.program_id(1)
    nu = nused_ref[0]

    def copies(e, jj, slot):
        return [pltpu.make_async_copy(
            w_hbm.at[layer, e, :, pl.ds(pl.multiple_of(c0 + jj * width, width), width)],
            stage.at[slot, n], sem.at[slot, n]) for n, c0 in enumerate(col_starts)]

    @pl.when((j == 0) & (i == 0))
    def _():
        cnt_ref[0] = 0
        for c in copies(be_ref[0], 0, 0):
            c.start()

    @pl.when((i < nu) & _expert_changed(be_ref, i))
    def _():
        k = cnt_ref[0]
        slot = k % 2
        for c in copies(be_ref[i], j, slot):
            c.wait()
        for n, dst in enumerate(dsts):
            dst[...] = stage[slot, n].astype(BF16)
        cnt_ref[0] = k + 1
        jn = jnp.where(be_ref[i] == be_ref[nu - 1], j + 1, j)

        @pl.when(jn < pl.num_programs(0))
        def _():
            for c in copies(nxt_ref[i], jn, 1 - slot):
                c.start()


def _expert_up_kernel(layer, be_ref, nxt_ref, nused_ref, x_ref, bg_ref, bu_ref, w_hbm, h_ref,
                      stage, wg_scr, wu_scr, cnt_ref, sem):
    i = pl.program_id(1)
    active = i < nused_ref[0]
    _stream_weight_tile(be_ref, nxt_ref, nused_ref, w_hbm, layer, (0, D_FF), MOE_TF, stage,
                        (wg_scr, wu_scr), cnt_ref, sem)

    @pl.when(active)
    def _():
        x = x_ref[...]
        gate = jnp.dot(x, wg_scr[...], preferred_element_type=F32) + bg_ref[0, 0]
        up = jnp.dot(x, wu_scr[...], preferred_element_type=F32) + bu_ref[0, 0]
        gate = jnp.minimum(gate, SWIGLU_LIMIT)
        up = jnp.clip(up, -SWIGLU_LIMIT, SWIGLU_LIMIT)
        h_ref[...] = ((up + 1.0) * (gate * jax.nn.sigmoid(SWIGLU_ALPHA * gate))).astype(h_ref.dtype)

    @pl.when(i >= nused_ref[0])
    def _():
        h_ref[...] = jnp.zeros_like(h_ref)


def _expert_up(xg, w_gu, b_gu, blk_e, nxt_e, n_used, layer):
    nf = D_FF // MOE_TF

    def used(i, nu):
        return jnp.minimum(i, nu[0] - 1)

    def bspec(col0):
        return pl.BlockSpec((1, 1, 1, MOE_TF),
                            lambda j, i, be, nx, nu: (layer, be[used(i, nu)], 0, col0 + j))

    return pl.pallas_call(
        functools.partial(_expert_up_kernel, layer),
        grid_spec=pltpu.PrefetchScalarGridSpec(
            num_scalar_prefetch=3,
            grid=(nf, MOE_NB),
            in_specs=[
                pl.BlockSpec((MOE_TB, D_MODEL), lambda j, i, be, nx, nu: (used(i, nu), 0)),
                bspec(0), bspec(nf), pl.BlockSpec(memory_space=pl.ANY),
            ],
            out_specs=pl.BlockSpec((MOE_TB, MOE_TF), lambda j, i, be, nx, nu: (i, j)),
            scratch_shapes=[pltpu.VMEM((2, 2, D_MODEL, MOE_TF), F32),
                            pltpu.VMEM((D_MODEL, MOE_TF), BF16), pltpu.VMEM((D_MODEL, MOE_TF), BF16),
                            pltpu.SMEM((1,), jnp.int32), pltpu.SemaphoreType.DMA((2, 2))],
        ),
        out_shape=jax.ShapeDtypeStruct((MOE_P, D_FF), BF16),
        compiler_params=_cparams(("arbitrary", "arbitrary"), VMEM_LIMIT),
        name="expert_up",
    )(blk_e, nxt_e, n_used, xg, b_gu, b_gu, w_gu)


def _expert_down_kernel(layer, be_ref, nxt_ref, nused_ref, h_ref, b_ref, w_hbm, y_ref,
                        stage, w_scr, cnt_ref, sem):
    i = pl.program_id(1)
    active = i < nused_ref[0]
    _stream_weight_tile(be_ref, nxt_ref, nused_ref, w_hbm, layer, (0,), MOE_TN, stage, (w_scr,),
                        cnt_ref, sem)

    @pl.when(active)
    def _():
        y_ref[...] = jnp.dot(h_ref[...], w_scr[...], preferred_element_type=F32) + b_ref[0, 0]

    @pl.when(i >= nused_ref[0])
    def _():
        y_ref[...] = jnp.zeros_like(y_ref)


def _expert_down(h, w_dn, b_dn, blk_e, nxt_e, n_used, layer):
    nn = D_MODEL // MOE_TN

    def used(i, nu):
        return jnp.minimum(i, nu[0] - 1)

    return pl.pallas_call(
        functools.partial(_expert_down_kernel, layer),
        grid_spec=pltpu.PrefetchScalarGridSpec(
            num_scalar_prefetch=3,
            grid=(nn, MOE_NB),
            in_specs=[
                pl.BlockSpec((MOE_TB, D_FF), lambda j, i, be, nx, nu: (used(i, nu), 0)),
                pl.BlockSpec((1, 1, 1, MOE_TN),
                             lambda j, i, be, nx, nu: (layer, be[used(i, nu)], 0, j)),
                pl.BlockSpec(memory_space=pl.ANY),
            ],
            out_specs=pl.BlockSpec((MOE_TB, MOE_TN), lambda j, i, be, nx, nu: (i, j)),
            scratch_shapes=[pltpu.VMEM((2, 1, D_FF, MOE_TN), F32), pltpu.VMEM((D_FF, MOE_TN), BF16),
                            pltpu.SMEM((1,), jnp.int32), pltpu.SemaphoreType.DMA((2, 1))],
        ),
        out_shape=jax.ShapeDtypeStruct((MOE_P, D_MODEL), F32),
        compiler_params=_cparams(("arbitrary", "arbitrary"), VMEM_LIMIT),
        name="expert_down",
    )(blk_e, nxt_e, n_used, h, b_dn, w_dn)


def _combine_kernel(pos_ref, pos_next_ref, gates_ref, x1_ref, y_hbm, g_ref, b_ref, x2_ref, x2b_ref,
                    buf, sem):
    i = pl.program_id(0)
    slot = i % 2
    n = CMB_T * TOP_K

    @pl.when(i == 0)
    def _():
        _issue_rows(y_hbm, pos_ref, n, buf.at[0], sem.at[0])

    @pl.when(i + 1 < pl.num_programs(0))
    def _():
        _issue_rows(y_hbm, pos_next_ref, n, buf.at[1 - slot], sem.at[1 - slot])

    _wait_rows(y_hbm, n, buf.at[slot], sem.at[slot])
    gates = gates_ref[...]
    y = jnp.zeros((CMB_T, D_MODEL), F32)
    for k in range(TOP_K):
        y = y + buf[slot, k * CMB_T:(k + 1) * CMB_T, :] * gates[:, k:k + 1]
    x2 = _layernorm(DN_ALPHA * x1_ref[...] + y, g_ref[...], b_ref[...])
    x2_ref[...] = x2
    x2b_ref[...] = x2.astype(BF16)


def _moe_combine(pos_kmajor, gates, x1, y, g, b):
    row = pl.BlockSpec((CMB_T, D_MODEL), lambda i: (i, 0))
    vec = pl.BlockSpec((1, D_MODEL), lambda i: (0, 0))
    steps = N_TOK // CMB_T
    return pl.pallas_call(
        _combine_kernel,
        grid=(steps,),
        in_specs=[pl.BlockSpec((CMB_T * TOP_K,), lambda i: (i,), memory_space=pltpu.SMEM),
                  pl.BlockSpec((CMB_T * TOP_K,), lambda i: (jnp.minimum(i + 1, steps - 1),),
                               memory_space=pltpu.SMEM),
                  pl.BlockSpec((CMB_T, TOP_K), lambda i: (i, 0)),
                  row, pl.BlockSpec(memory_space=pl.ANY), vec, vec],
        out_specs=[row, row],
        out_shape=[jax.ShapeDtypeStruct((N_TOK, D_MODEL), F32),
                   jax.ShapeDtypeStruct((N_TOK, D_MODEL), BF16)],
        scratch_shapes=[pltpu.VMEM((2, CMB_T * TOP_K, D_MODEL), F32),
                        pltpu.SemaphoreType.DMA((2,))],
        compiler_params=_cparams(("arbitrary",)),
        name="moe_combine",
    )(pos_kmajor, pos_kmajor, gates, x1, y, g, b)


def _permute_w_in(w):
    parts, off = [], 0
    for n in REF_IN_SIZES:
        parts.append((off, n))
        off += n
    return jnp.concatenate([w[:, parts[i][0]:parts[i][0] + parts[i][1]] for i in MY_ORDER],
                           axis=1).astype(BF16)


def kernel(x_prompt, x_sample, cache_conv, state_ret, cache_swa_k, cache_swa_v, cache_mem_k, cache_mem_v, mem_prompt, w_in, conv_w, ret_gn_w, attn_sinks, w_mem_kv, w_o, ln1_g, ln1_b, router_w, router_b, w_gate_up, b_gate_up, w_down, b_down, ln2_g, ln2_b):
    pos_p = jnp.arange(SEQ, dtype=jnp.int32)
    pos_s = PAST_LEN + jnp.arange(DEC_SEQ, dtype=jnp.int32)
    ret_tabs_p = _ret_tables(pos_p, R_CHUNK)
    ret_tabs_s = _ret_tables(pos_s, DEC_SEQ)
    swa_tabs_p = _swa_tables(pos_p)
    swa_tabs_s = _swa_tables(pos_s)
    kvw = A_KV * A_HD

    xp = x_prompt.reshape(N_PROMPT, D_MODEL)
    xs = x_sample.reshape(N_SAMPLE, D_MODEL)
    xb = jnp.concatenate([xp, xs], axis=0).astype(BF16)
    mem_b = mem_prompt.reshape(BATCH * N_MEM, D_MODEL).astype(BF16)
    zeros_e = jnp.zeros((256, D_MODEL), F32)
    b_gu = b_gate_up.reshape(DEPTH, N_EXP, 1, 2 * D_FF)
    b_dn = b_down.reshape(DEPTH, N_EXP, 1, D_MODEL)
    swa_k = cache_swa_k.reshape(DEPTH, DEC_BATCH, WINDOW, kvw)
    swa_v = cache_swa_v.reshape(DEPTH, DEC_BATCH, WINDOW, kvw)

    outs = {k: [] for k in ("p_conv", "p_ret", "p_k", "p_v", "p_mk", "p_mv",
                            "s_conv", "s_ret", "s_k", "s_v")}
    x2 = None
    for l in range(DEPTH):
        z = _matmul(xb, _permute_w_in(w_in[l]), 2304, 512)
        gn = ret_gn_w[l].reshape(1, D_MODEL)
        sinks = attn_sinks[l]

        ya_p, tail_p = _conv_branch(z, conv_w[l], zeros_e, zeros_e, 0, N_PROMPT, SEQ, False)
        yb_p, ret_p = _ret_prompt(z, ret_tabs_p, gn)
        yc_p, kr_p = _swa_prompt(z, sinks, swa_tabs_p)
        mkv = _matmul(mem_b, w_mem_kv[l].astype(BF16), 512, 512)
        mk_p = mkv[:, :D_MODEL].reshape(BATCH, N_MEM, D_MODEL)
        mv_p = mkv[:, D_MODEL:].reshape(BATCH, N_MEM, D_MODEL)
        ym_p = _mem_attend(z, mk_p, mv_p, 0, BATCH, SEQ, 512, "mem_prompt")

        ea = jnp.repeat(cache_conv[l][:, 0], DEC_SEQ, axis=0)
        eb = jnp.repeat(cache_conv[l][:, 1], DEC_SEQ, axis=0)
        ya_s, u_s = _conv_branch(z, conv_w[l], ea, eb, N_PROMPT, N_SAMPLE, DEC_SEQ, True)
        yb_s, ret_s = _ret_sample(z, ret_tabs_s, gn, state_ret, l)
        yc_s, nk_s, nv_s = _swa_sample(z, sinks, swa_tabs_s, swa_k, swa_v, l)
        ym_s = _mem_attend(z, cache_mem_k, cache_mem_v, N_PROMPT, DEC_BATCH, DEC_SEQ, DEC_SEQ,
                           "mem_sample", layer=l)

        wo_b = w_o[l].astype(BF16)
        g1 = ln1_g[l].reshape(1, D_MODEL)
        b1 = ln1_b[l].reshape(1, D_MODEL)
        rw = jnp.pad(router_w[l], ((0, 0), (0, LANES - N_EXP))).astype(BF16)
        rb = jnp.pad(router_b[l], (0, LANES - N_EXP)).reshape(1, LANES)
        x_p, x_s, xs_row0 = (xp, xs, 0) if l == 0 else (x2, x2, N_PROMPT)
        x1, lg = _wo_ln((ya_p, yb_p, yc_p, ym_p), x_p, (ya_s, yb_s, yc_s, ym_s), x_s, xs_row0,
                        wo_b, g1, b1, rw, rb)
        logits = lg[:, :N_EXP]

        gates, row_tok, pos, blk_e, nxt_e, n_used = _route(logits)
        xg = _moe_gather(x1, row_tok, n_used)
        h = _expert_up(xg, w_gate_up, b_gu, blk_e, nxt_e, n_used, l)
        y = _expert_down(h, w_down, b_dn, blk_e, nxt_e, n_used, l)
        pos_k = pos.reshape(N_TOK // CMB_T, CMB_T, TOP_K).transpose(0, 2, 1).reshape(-1)
        x2, xb = _moe_combine(pos_k, gates, x1, y, ln2_g[l].reshape(1, D_MODEL),
                              ln2_b[l].reshape(1, D_MODEL))

        tail_p = tail_p.reshape(BATCH, SEQ // 256, SUBLANES, D_MODEL)
        outs["p_conv"].append(tail_p[:, -1, -(CONV_W - 1):])
        outs["p_ret"].append(ret_p)
        outs["p_k"].append(kr_p.reshape(BATCH, SEQ, A_KV, A_HD)[:, -WINDOW:])
        outs["p_v"].append(z[:N_PROMPT, AV:AV + kvw].reshape(BATCH, SEQ, A_KV, A_HD)[:, -WINDOW:])
        outs["p_mk"].append(mk_p.reshape(BATCH, N_MEM, M_HEADS, M_HD))
        outs["p_mv"].append(mv_p.reshape(BATCH, N_MEM, M_HEADS, M_HD))
        outs["s_conv"].append(u_s.reshape(DEC_BATCH, DEC_SEQ, D_MODEL)[:, -(CONV_W - 1):])
        outs["s_ret"].append(ret_s)
        outs["s_k"].append(nk_s.reshape(DEC_BATCH, WINDOW, A_KV, A_HD))
        outs["s_v"].append(nv_s.reshape(DEC_BATCH, WINDOW, A_KV, A_HD))

    return (x2[:N_PROMPT].reshape(BATCH, SEQ, D_MODEL),
            x2[N_PROMPT:].reshape(DEC_BATCH, DEC_SEQ, D_MODEL),
            jnp.stack(outs["p_conv"]), jnp.stack(outs["p_ret"]), jnp.stack(outs["p_k"]),
            jnp.stack(outs["p_v"]), jnp.stack(outs["p_mk"]), jnp.stack(outs["p_mv"]),
            jnp.stack(outs["s_conv"]), jnp.stack(outs["s_ret"]), jnp.stack(outs["s_k"]),
            jnp.stack(outs["s_v"]))
```

```python
import functools

import jax
import jax.numpy as jnp
from jax import lax
from jax.experimental import pallas as pl
from jax.experimental.pallas import tpu as pltpu

F32 = jnp.float32
BF16 = jnp.bfloat16

D_MODEL = 2048
BATCH = 2
SEQ = 4096
DEPTH = 2
DEC_BATCH = 128
DEC_SEQ = 8
PAST_LEN = 8192
N_PROMPT = BATCH * SEQ
N_SAMPLE = DEC_BATCH * DEC_SEQ
N_TOK = N_PROMPT + N_SAMPLE

CONV_W = 3
R_HEADS = 8
R_DK = 128
R_DV = D_MODEL // R_HEADS
R_CHUNK = 128
R_THETA = 10000.0
A_HEADS = 32
A_KV = 4
A_HD = D_MODEL // A_HEADS
A_GROUP = A_HEADS // A_KV
WINDOW = 128
ROPE_DIMS = A_HD // 4
ROPE_THETA = 500000.0
N_MEM = 256
M_HEADS = 4
M_HD = D_MODEL // M_HEADS
N_EXP = 32
TOP_K = 4
D_FF = D_MODEL
SWIGLU_LIMIT = 7.0
SWIGLU_ALPHA = 1.702
LN_EPS = 1e-5
GN_EPS = 1e-6
DN_ALPHA = (2 * DEPTH) ** 0.25

REF_IN_SIZES = (D_MODEL, D_MODEL, D_MODEL, R_HEADS * R_DK, R_HEADS * R_DK, D_MODEL, D_MODEL,
                D_MODEL, A_KV * A_HD, A_KV * A_HD, D_MODEL, 4 * D_MODEL)
D_IN = sum(REF_IN_SIZES)
MY_ORDER = (11, 0, 1, 2, 5, 6, 7, 10, 3, 4, 8, 9)
GL, HC, BC, CC, RV, RG, AQ, MQ, RQ, RK, AK, AV = (
    0, 8192, 10240, 12288, 14336, 16384, 18432, 20480, 22528, 23552, 24576, 24832)

LANES = 128
SUBLANES = 8
VMEM_LIMIT = 56 * 1024 * 1024

MOE_TB = 256
MOE_NB = N_TOK * TOP_K // MOE_TB + N_EXP
MOE_P = MOE_NB * MOE_TB
MOE_TF = 1024
MOE_TN = 2048
CMB_T = 64


def _cparams(sem, vmem=None):
    return pltpu.CompilerParams(dimension_semantics=sem, vmem_limit_bytes=vmem)


def _mm_kernel(x_ref, w_ref, o_ref):
    o_ref[...] = jnp.dot(x_ref[...], w_ref[...], preferred_element_type=F32).astype(o_ref.dtype)


def _matmul(x, w, tm, tn, out_dtype=F32):
    m, k = x.shape
    n = w.shape[1]
    return pl.pallas_call(
        _mm_kernel,
        grid=(m // tm, n // tn),
        in_specs=[pl.BlockSpec((tm, k), lambda i, j: (i, 0)),
                  pl.BlockSpec((k, tn), lambda i, j: (0, j))],
        out_specs=pl.BlockSpec((tm, tn), lambda i, j: (i, j)),
        out_shape=jax.ShapeDtypeStruct((m, n), out_dtype),
        compiler_params=_cparams(("parallel", "parallel"), VMEM_LIMIT),
        name="matmul",
    )(x, w)


def _conv_kernel(seq_len, full_u, hc_ref, cc_ref, bc_ref, g_ref, hh_ref, ch_ref, ea_ref, eb_ref,
                 w_ref, ya_ref, tail_ref):
    i = pl.program_id(0)
    tm = hc_ref.shape[0]
    u = cc_ref[...] * hc_ref[...]
    ue = jnp.concatenate([ch_ref[...] * hh_ref[...], u], axis=0)
    p1 = pltpu.roll(ue, 1, 0)[SUBLANES:]
    p2 = pltpu.roll(ue, 2, 0)[SUBLANES:]
    t = (i * tm + lax.broadcasted_iota(jnp.int32, (tm, 1), 0)) % seq_len
    ea = ea_ref[...]
    eb = eb_ref[...]
    p1 = jnp.where(t == 0, eb, p1)
    p2 = jnp.where(t == 0, ea, jnp.where(t == 1, eb, p2))
    w = w_ref[...]
    y = p2 * w[0:1] + p1 * w[1:2] + u * w[2:3]
    ya_ref[...] = jax.nn.sigmoid(g_ref[...]) * (bc_ref[...] * y)
    tail_ref[...] = u if full_u else u[tm - SUBLANES:]


def _conv_branch(z, conv_w, ea, eb, row0, rows, seq_len, sample):
    tm, tc = 256, 512
    r0 = row0 // tm
    nj = D_MODEL // tc

    def zspec(col):
        return pl.BlockSpec((tm, tc), lambda i, j: (r0 + i, col // tc + j))

    def halo(col):
        return pl.BlockSpec(
            (SUBLANES, tc),
            lambda i, j: (jnp.maximum((r0 + i) * (tm // SUBLANES) - 1, 0), col // tc + j))

    if sample:
        espec = pl.BlockSpec((tm, tc), lambda i, j: (i, j))
        tail_spec = pl.BlockSpec((tm, tc), lambda i, j: (i, j))
        tail_shape = jax.ShapeDtypeStruct((rows, D_MODEL), F32)
    else:
        espec = pl.BlockSpec((tm, tc), lambda i, j: (0, j))
        tail_spec = pl.BlockSpec((SUBLANES, tc), lambda i, j: (i, j))
        tail_shape = jax.ShapeDtypeStruct((rows // tm * SUBLANES, D_MODEL), F32)
    return pl.pallas_call(
        functools.partial(_conv_kernel, seq_len, sample),
        grid=(rows // tm, nj),
        in_specs=[zspec(HC), zspec(CC), zspec(BC), zspec(GL), halo(HC), halo(CC), espec, espec,
                  pl.BlockSpec((CONV_W, tc), lambda i, j: (0, j))],
        out_specs=[pl.BlockSpec((tm, tc), lambda i, j: (i, j)), tail_spec],
        out_shape=[jax.ShapeDtypeStruct((rows, D_MODEL), F32), tail_shape],
        compiler_params=_cparams(("parallel", "parallel"), VMEM_LIMIT),
        name="conv_sample" if sample else "conv_prompt",
    )(z, z, z, z, z, z, ea, eb, conv_w)


def _rope_full(x, cos, sin):
    return x * cos + pltpu.roll(x, R_DK // 2, 1) * sin


def _bf16_exact(x):
    return x.astype(BF16).astype(F32)


def _group_norm_gate(o, gn, rg, g):
    mu = jnp.mean(o, axis=-1, keepdims=True)
    d = o - mu
    var = jnp.mean(d * d, axis=-1, keepdims=True)
    yn = d * lax.rsqrt(var + GN_EPS) * gn
    return jax.nn.sigmoid(g) * ((rg * jax.nn.sigmoid(rg)) * yn)


def _ret_prompt_kernel(q_ref, k_ref, v_ref, rg_ref, g_ref, cos_ref, sin_ref, dmask_ref, qdec_ref,
                       kdec_ref, cdec_ref, gn_ref, y_ref, sfin_ref, s_scr):
    c = pl.program_id(1)

    @pl.when(c == 0)
    def _():
        s_scr[...] = jnp.zeros_like(s_scr)

    cos = cos_ref[...]
    sin = sin_ref[...]
    for h in range(R_HEADS):
        qs = slice(h * R_DK, (h + 1) * R_DK)
        vs = slice(h * R_DV, (h + 1) * R_DV)
        q = _rope_full(q_ref[:, qs], cos, sin)
        k = _rope_full(k_ref[:, qs], cos, sin) * (R_DK ** -0.5)
        vb = v_ref[:, vs].astype(BF16)
        s = s_scr[h]
        att = lax.dot_general(q.astype(BF16), k.astype(BF16), (((1,), (1,)), ((), ())),
                              preferred_element_type=F32) * dmask_ref[h]
        o = (jnp.dot(att.astype(BF16), vb, preferred_element_type=F32)
             + jnp.dot((q * qdec_ref[h]).astype(BF16), s.astype(BF16), preferred_element_type=F32))
        kd = (k * kdec_ref[h]).astype(BF16)
        s_new = cdec_ref[h] * s + lax.dot_general(kd, vb, (((0,), (0,)), ((), ())),
                                                  preferred_element_type=F32)
        s_scr[h] = s_new
        sfin_ref[0, h] = s_new
        y_ref[:, vs] = _group_norm_gate(o, gn_ref[:, vs], rg_ref[:, vs], g_ref[:, vs])


def _ret_prompt(z, tabs, gn_w):
    cos, sin, dmask, qdec, kdec, cdec = tabs
    nc = SEQ // R_CHUNK

    def zspec(col, width):
        return pl.BlockSpec((R_CHUNK, width), lambda b, c: (b * nc + c, col // width))

    def full(a):
        return pl.BlockSpec(a.shape, lambda b, c: (0,) * a.ndim)

    tspec = pl.BlockSpec((R_CHUNK, R_DK), lambda b, c: (c, 0))
    return pl.pallas_call(
        _ret_prompt_kernel,
        grid=(BATCH, nc),
        in_specs=[zspec(RQ, R_HEADS * R_DK), zspec(RK, R_HEADS * R_DK), zspec(RV, D_MODEL),
                  zspec(RG, D_MODEL), zspec(GL + D_MODEL, D_MODEL), tspec, tspec,
                  full(dmask), full(qdec), full(kdec), full(cdec), full(gn_w)],
        out_specs=[pl.BlockSpec((R_CHUNK, D_MODEL), lambda b, c: (b * nc + c, 0)),
                   pl.BlockSpec((1, R_HEADS, R_DK, R_DV), lambda b, c: (b, 0, 0, 0))],
        out_shape=[jax.ShapeDtypeStruct((N_PROMPT, D_MODEL), F32),
                   jax.ShapeDtypeStruct((BATCH, R_HEADS, R_DK, R_DV), F32)],
        scratch_shapes=[pltpu.VMEM((R_HEADS, R_DK, R_DV), F32)],
        compiler_params=_cparams(("parallel", "arbitrary"), VMEM_LIMIT),
        name="retention_prompt",
    )(z, z, z, z, z, cos, sin, dmask, qdec, kdec, cdec, gn_w)


def _ret_sample_kernel(q_ref, k_ref, v_ref, rg_ref, g_ref, cos_ref, sin_ref, dmask_ref, qdec_ref,
                       kdec_ref, cdec_ref, gn_ref, s_ref, y_ref, snew_ref):
    cos = cos_ref[...]
    sin = sin_ref[...]
    for h in range(R_HEADS):
        qs = slice(h * R_DK, (h + 1) * R_DK)
        vs = slice(h * R_DV, (h + 1) * R_DV)
        q = _rope_full(q_ref[:, qs], cos, sin)
        k = _rope_full(k_ref[:, qs], cos, sin) * (R_DK ** -0.5)
        v = _bf16_exact(v_ref[:, vs])
        s = s_ref[0, 0, h]
        att = lax.dot_general(_bf16_exact(q), _bf16_exact(k), (((1,), (1,)), ((), ())),
                              preferred_element_type=F32) * dmask_ref[h]
        o = (jnp.dot(_bf16_exact(att), v, preferred_element_type=F32)
             + jnp.dot((q * qdec_ref[h]).astype(BF16), s.astype(BF16), preferred_element_type=F32))
        kd = _bf16_exact(k * kdec_ref[h])
        snew_ref[0, h] = cdec_ref[h] * s + lax.dot_general(
            kd, v, (((0,), (0,)), ((), ())), preferred_element_type=F32)
        y_ref[:, vs] = _group_norm_gate(o, gn_ref[:, vs], rg_ref[:, vs], g_ref[:, vs])


def _ret_sample(z, tabs, gn_w, state, layer):
    cos, sin, dmask, qdec, kdec, cdec = tabs
    r0 = N_PROMPT // DEC_SEQ

    def zspec(col, width):
        return pl.BlockSpec((DEC_SEQ, width), lambda b: (r0 + b, col // width))

    def full(a):
        return pl.BlockSpec(a.shape, lambda b: (0,) * a.ndim)

    sspec = pl.BlockSpec((1, R_HEADS, R_DK, R_DV), lambda b: (b, 0, 0, 0))
    sin_spec = pl.BlockSpec((1, 1, R_HEADS, R_DK, R_DV), lambda b: (layer, b, 0, 0, 0))
    return pl.pallas_call(
        _ret_sample_kernel,
        grid=(DEC_BATCH,),
        in_specs=[zspec(RQ, R_HEADS * R_DK), zspec(RK, R_HEADS * R_DK), zspec(RV, D_MODEL),
                  zspec(RG, D_MODEL), zspec(GL + D_MODEL, D_MODEL),
                  full(cos), full(sin), full(dmask), full(qdec), full(kdec), full(cdec), full(gn_w),
                  sin_spec],
        out_specs=[pl.BlockSpec((DEC_SEQ, D_MODEL), lambda b: (b, 0)), sspec],
        out_shape=[jax.ShapeDtypeStruct((N_SAMPLE, D_MODEL), F32),
                   jax.ShapeDtypeStruct(state.shape[1:], F32)],
        compiler_params=_cparams(("parallel",)),
        name="retention_sample",
    )(z, z, z, z, z, cos, sin, dmask, qdec, kdec, cdec, gn_w, state)


def _ret_tables(pos, chunk):
    half = R_DK // 2
    inv = R_THETA ** (-jnp.arange(half, dtype=F32) / half)
    ang = pos.astype(F32)[:, None] * inv[None, :]
    cos = jnp.concatenate([jnp.cos(ang), jnp.cos(ang)], axis=-1)
    sin = jnp.concatenate([-jnp.sin(ang), jnp.sin(ang)], axis=-1)
    lg = jnp.log(1.0 - 2.0 ** (-5.0 - jnp.arange(R_HEADS, dtype=F32)))
    i = jnp.arange(chunk, dtype=F32)
    diff = i[:, None] - i[None, :]
    dmask = jnp.where(diff >= 0, jnp.exp(lg[:, None, None] * jnp.maximum(diff, 0.0)), 0.0)
    qdec = jnp.broadcast_to(jnp.exp(lg[:, None] * (i + 1.0))[:, :, None], (R_HEADS, chunk, R_DK))
    kdec = jnp.broadcast_to(jnp.exp(lg[:, None] * (chunk - 1.0 - i))[:, :, None],
                            (R_HEADS, chunk, R_DK))
    cdec = jnp.broadcast_to(jnp.exp(lg * chunk)[:, None, None], (R_HEADS, 1, R_DV))
    return cos, sin, dmask.astype(F32), qdec.astype(F32), kdec.astype(F32), cdec.astype(F32)


def _swa_tables(pos):
    half = ROPE_DIMS // 2
    inv = ROPE_THETA ** (-jnp.arange(half, dtype=F32) / half)
    ang = pos.astype(F32)[:, None] * inv[None, :]
    cos, sin = jnp.cos(ang), jnp.sin(ang)
    t = pos.shape[0]
    ones = jnp.ones((t, A_HD - ROPE_DIMS), F32)
    zeros = jnp.zeros((t, A_HD - ROPE_DIMS), F32)
    zh = jnp.zeros((t, half), F32)
    c = jnp.concatenate([cos, cos, ones], axis=-1)
    s_lo = jnp.concatenate([-sin, zh, zeros], axis=-1)
    s_hi = jnp.concatenate([zh, sin, zeros], axis=-1)
    rep = LANES // A_HD
    return jnp.tile(c, (1, rep)), jnp.tile(s_lo, (1, rep)), jnp.tile(s_hi, (1, rep))


def _rope_partial(x, c, s_lo, s_hi):
    half = ROPE_DIMS // 2
    return x * c + pltpu.roll(x, LANES - half, 1) * s_lo + pltpu.roll(x, half, 1) * s_hi


def _dup_head(chunk, odd):
    lane = lax.broadcasted_iota(jnp.int32, chunk.shape, 1)
    sw = pltpu.roll(chunk, A_HD, 1)
    if odd:
        return jnp.where(lane < A_HD, sw, chunk)
    return jnp.where(lane < A_HD, chunk, sw)


def _swa_core(sink_ref, q_ref, g_ref, y_ref, kp, kc, vp, vc, tabs_q, has_prev, tq, chunks_per_dot):
    cq, slq, shq = tabs_q
    lane_q = lax.broadcasted_iota(jnp.int32, (tq, LANES), 1)
    n_chunks = A_GROUP * A_HD // LANES
    for h in range(A_KV):
        kchunk = slice((h // 2) * LANES, (h // 2 + 1) * LANES)
        kkp = _dup_head(kp[:, kchunk], h % 2).astype(BF16)
        kkc = _dup_head(kc[:, kchunk], h % 2).astype(BF16)
        vvp = _dup_head(vp[:, kchunk], h % 2).astype(BF16)
        vvc = _dup_head(vc[:, kchunk], h % 2).astype(BF16)
        for c0 in range(0, n_chunks, chunks_per_dot):
            pieces, sinks = [], []
            for ci in range(c0, c0 + chunks_per_dot):
                col = h * A_GROUP * A_HD + ci * LANES
                qc = _rope_partial(q_ref[:, col:col + LANES], cq, slq, shq) * (A_HD ** -0.5)
                for half in range(2):
                    keep = (lane_q < A_HD) if half == 0 else (lane_q >= A_HD)
                    pieces.append(jnp.where(keep, qc, 0.0).astype(BF16))
                    head = h * A_GROUP + 2 * ci + half
                    sinks.append(jnp.full((tq, 1), sink_ref[head], F32))
            qs = jnp.concatenate(pieces, axis=0)
            sink = jnp.concatenate(sinks, axis=0)
            r = qs.shape[0]
            nt = (((1,), (1,)), ((), ()))
            sp = lax.dot_general(qs, kkp, nt, preferred_element_type=F32)
            sc = lax.dot_general(qs, kkc, nt, preferred_element_type=F32)
            t = lax.broadcasted_iota(jnp.int32, (r, WINDOW), 0) % tq
            j = lax.broadcasted_iota(jnp.int32, (r, WINDOW), 1)
            sp = jnp.where((j >= t) & has_prev, sp, -jnp.inf)
            sc = jnp.where(j <= t, sc, -jnp.inf)
            m = jnp.maximum(jnp.max(jnp.maximum(sp, sc), axis=-1, keepdims=True), sink)
            pp = jnp.exp(sp - m)
            pc = jnp.exp(sc - m)
            den = jnp.sum(pp + pc, axis=-1, keepdims=True) + jnp.exp(sink - m)
            inv = 1.0 / den
            o = (jnp.dot((pp * inv).astype(BF16), vvp, preferred_element_type=F32)
                 + jnp.dot((pc * inv).astype(BF16), vvc, preferred_element_type=F32))
            for n, ci in enumerate(range(c0, c0 + chunks_per_dot)):
                col = h * A_GROUP * A_HD + ci * LANES
                oa = o[(2 * n) * tq:(2 * n + 1) * tq]
                ob = o[(2 * n + 1) * tq:(2 * n + 2) * tq]
                oc = jnp.where(lane_q < A_HD, oa, ob)
                y_ref[:, col:col + LANES] = jax.nn.sigmoid(g_ref[:, col:col + LANES]) * oc


def _rope_kv(k_ref, ck, slk, shk):
    parts = [_rope_partial(k_ref[:, c * LANES:(c + 1) * LANES], ck, slk, shk)
             for c in range(A_KV * A_HD // LANES)]
    return jnp.concatenate(parts, axis=-1)


def _swa_prompt_kernel(sink_ref, q_ref, k_ref, v_ref, g_ref, c_ref, sl_ref, sh_ref,
                       y_ref, kr_ref, kp_scr, vp_scr):
    n = pl.program_id(1)

    @pl.when(n == 0)
    def _():
        kp_scr[...] = jnp.zeros_like(kp_scr)
        vp_scr[...] = jnp.zeros_like(vp_scr)

    tabs = (c_ref[...], sl_ref[...], sh_ref[...])
    kc = _rope_kv(k_ref, *tabs)
    vc = v_ref[...]
    kr_ref[...] = kc
    _swa_core(sink_ref, q_ref, g_ref, y_ref, kp_scr[...], kc, vp_scr[...], vc, tabs, n > 0,
              WINDOW, A_GROUP * A_HD // LANES)
    kp_scr[...] = kc
    vp_scr[...] = vc


def _swa_prompt(z, sinks, tabs):
    nb = SEQ // WINDOW
    kvw = A_KV * A_HD

    def zspec(col, width):
        return pl.BlockSpec((WINDOW, width), lambda b, n: (b * nb + n, col // width))

    tspec = pl.BlockSpec((WINDOW, LANES), lambda b, n: (n, 0))
    return pl.pallas_call(
        _swa_prompt_kernel,
        grid=(BATCH, nb),
        in_specs=[pl.BlockSpec(memory_space=pltpu.SMEM),
                  zspec(AQ, D_MODEL), zspec(AK, kvw), zspec(AV, kvw), zspec(GL + 2 * D_MODEL, D_MODEL),
                  tspec, tspec, tspec],
        out_specs=[pl.BlockSpec((WINDOW, D_MODEL), lambda b, n: (b * nb + n, 0)),
                   pl.BlockSpec((WINDOW, kvw), lambda b, n: (b * nb + n, 0))],
        out_shape=[jax.ShapeDtypeStruct((N_PROMPT, D_MODEL), F32),
                   jax.ShapeDtypeStruct((N_PROMPT, kvw), F32)],
        scratch_shapes=[pltpu.VMEM((WINDOW, kvw), F32), pltpu.VMEM((WINDOW, kvw), F32)],
        compiler_params=_cparams(("parallel", "arbitrary"), VMEM_LIMIT),
        name="swa_prompt",
    )(sinks, z, z, z, z, *tabs)


def _swa_sample_kernel(sink_ref, q_ref, k_ref, v_ref, g_ref, c_ref, sl_ref, sh_ref, ck_ref, cv_ref,
                       y_ref, nk_ref, nv_ref):
    tabs = (c_ref[...], sl_ref[...], sh_ref[...])
    kn = _rope_kv(k_ref, *tabs)
    vn = v_ref[...]
    pad = jnp.zeros((WINDOW - DEC_SEQ, A_KV * A_HD), F32)
    kp = ck_ref[0, 0]
    vp = cv_ref[0, 0]
    _swa_core(sink_ref, q_ref, g_ref, y_ref, kp, jnp.concatenate([kn, pad], axis=0),
              vp, jnp.concatenate([vn, pad], axis=0), tabs, True, DEC_SEQ,
              A_GROUP * A_HD // LANES)
    nk_ref[0] = jnp.concatenate([kp[DEC_SEQ:], kn], axis=0)
    nv_ref[0] = jnp.concatenate([vp[DEC_SEQ:], vn], axis=0)


def _swa_sample(z, sinks, tabs, cache_k, cache_v, layer):
    kvw = A_KV * A_HD
    r0 = N_PROMPT // DEC_SEQ

    def zspec(col, width):
        return pl.BlockSpec((DEC_SEQ, width), lambda b: (r0 + b, col // width))

    tspec = pl.BlockSpec((DEC_SEQ, LANES), lambda b: (0, 0))
    cspec = pl.BlockSpec((1, WINDOW, kvw), lambda b: (b, 0, 0))
    cin_spec = pl.BlockSpec((1, 1, WINDOW, kvw), lambda b: (layer, b, 0, 0))
    return pl.pallas_call(
        _swa_sample_kernel,
        grid=(DEC_BATCH,),
        in_specs=[pl.BlockSpec(memory_space=pltpu.SMEM),
                  zspec(AQ, D_MODEL), zspec(AK, kvw), zspec(AV, kvw), zspec(GL + 2 * D_MODEL, D_MODEL),
                  tspec, tspec, tspec, cin_spec, cin_spec],
        out_specs=[pl.BlockSpec((DEC_SEQ, D_MODEL), lambda b: (b, 0)), cspec, cspec],
        out_shape=[jax.ShapeDtypeStruct((N_SAMPLE, D_MODEL), F32),
                   jax.ShapeDtypeStruct((DEC_BATCH, WINDOW, kvw), F32),
                   jax.ShapeDtypeStruct((DEC_BATCH, WINDOW, kvw), F32)],
        compiler_params=_cparams(("parallel",)),
        name="swa_sample",
    )(sinks, z, z, z, z, *tabs, cache_k, cache_v)


def _mem_kernel(q_ref, k_ref, v_ref, g_ref, y_ref):
    for h in range(M_HEADS):
        hs = slice(h * M_HD, (h + 1) * M_HD)
        q = q_ref[:, hs].astype(BF16)
        k = k_ref[0, :, hs].astype(BF16)
        v = v_ref[0, :, hs].astype(BF16)
        s = lax.dot_general(q, k, (((1,), (1,)), ((), ())), preferred_element_type=F32) * (M_HD ** -0.5)
        m = jnp.max(s, axis=-1, keepdims=True)
        p = jnp.exp(s - m)
        p = p / jnp.sum(p, axis=-1, keepdims=True)
        o = jnp.dot(p.astype(BF16), v, preferred_element_type=F32)
        y_ref[:, hs] = jax.nn.sigmoid(g_ref[:, hs]) * o


def _mem_cache_kernel(q_ref, k_ref, v_ref, g_ref, y_ref):
    k2 = k_ref.reshape(N_MEM * M_HEADS, M_HD)
    v2 = v_ref.reshape(N_MEM * M_HEADS, M_HD)
    tq = q_ref.shape[0]
    qs = jnp.concatenate([q_ref[:, h * M_HD:(h + 1) * M_HD] for h in range(M_HEADS)],
                         axis=0).astype(BF16)
    s = lax.dot_general(qs, k2[...].astype(BF16), (((1,), (1,)), ((), ())),
                        preferred_element_type=F32) * (M_HD ** -0.5)
    row_h = lax.broadcasted_iota(jnp.int32, s.shape, 0) // tq
    col_h = lax.broadcasted_iota(jnp.int32, s.shape, 1) % M_HEADS
    s = jnp.where(row_h == col_h, s, -jnp.inf)
    m = jnp.max(s, axis=-1, keepdims=True)
    p = jnp.exp(s - m)
    p = p / jnp.sum(p, axis=-1, keepdims=True)
    o = jnp.dot(p.astype(BF16), v2[...].astype(BF16), preferred_element_type=F32)
    for h in range(M_HEADS):
        hs = slice(h * M_HD, (h + 1) * M_HD)
        y_ref[:, hs] = jax.nn.sigmoid(g_ref[:, hs]) * o[h * tq:(h + 1) * tq]


def _mem_attend(z, mk, mv, row0, nb, rows_per_b, tq, name, layer=None):
    nq = rows_per_b // tq
    r0 = row0 // tq

    def zspec(col):
        return pl.BlockSpec((tq, D_MODEL), lambda b, i: (r0 + b * nq + i, col // D_MODEL))

    if layer is None:
        kvspec = pl.BlockSpec((1, N_MEM, D_MODEL), lambda b, i: (b, 0, 0))
    else:
        kvspec = pl.BlockSpec((None, None, N_MEM, M_HEADS, M_HD),
                              lambda b, i: (layer, b, 0, 0, 0))
    return pl.pallas_call(
        _mem_kernel if layer is None else _mem_cache_kernel,
        grid=(nb, nq),
        in_specs=[zspec(MQ), kvspec, kvspec, zspec(GL + 3 * D_MODEL)],
        out_specs=pl.BlockSpec((tq, D_MODEL), lambda b, i: (b * nq + i, 0)),
        out_shape=jax.ShapeDtypeStruct((nb * rows_per_b, D_MODEL), F32),
        compiler_params=_cparams(("parallel", "parallel"), VMEM_LIMIT),
        name=name,
    )(z, mk, mv, z)


def _layernorm(r, g, b):
    mu = jnp.mean(r, axis=-1, keepdims=True)
    d = r - mu
    var = jnp.mean(d * d, axis=-1, keepdims=True)
    return d * lax.rsqrt(var + LN_EPS) * g + b


def _wo_ln_kernel(ya_ref, yb_ref, yc_ref, ym_ref, x_ref, wo_ref, g_ref, b_ref, rw_ref, rb_ref,
                  x1_ref, lg_ref):
    merged = ya_ref[...] + yb_ref[...] + yc_ref[...] + ym_ref[...]
    r = DN_ALPHA * x_ref[...] + jnp.dot(merged.astype(BF16), wo_ref[...], preferred_element_type=F32)
    x1 = _layernorm(r, g_ref[...], b_ref[...])
    x1_ref[...] = x1
    lg_ref[...] = jnp.dot(x1.astype(BF16), rw_ref[...], preferred_element_type=F32) + rb_ref[...]


def _wo_ln_both_kernel(n_prompt_tiles, *refs):
    i = pl.program_id(0)
    prompt, sample, rest = refs[:5], refs[5:10], refs[10:]

    @pl.when(i < n_prompt_tiles)
    def _():
        _wo_ln_kernel(*prompt, *rest)

    @pl.when(i >= n_prompt_tiles)
    def _():
        _wo_ln_kernel(*sample, *rest)


def _wo_ln(branches_p, x_p, branches_s, x_s, xs_row0, wo, g, b, rw, rb):
    tm = 128
    npt = N_PROMPT // tm
    xs0 = xs_row0 // tm
    p_row = pl.BlockSpec((tm, D_MODEL), lambda i: (jnp.minimum(i, npt - 1), 0))
    s_row = pl.BlockSpec((tm, D_MODEL), lambda i: (jnp.maximum(i - npt, 0), 0))
    sx_row = pl.BlockSpec((tm, D_MODEL), lambda i: (xs0 + jnp.maximum(i - npt, 0), 0))

    def full(a):
        return pl.BlockSpec(a.shape, lambda i: (0,) * a.ndim)

    return pl.pallas_call(
        functools.partial(_wo_ln_both_kernel, npt),
        grid=(N_TOK // tm,),
        in_specs=[p_row] * 5 + [s_row] * 4 + [sx_row]
        + [full(wo), full(g), full(b), full(rw), full(rb)],
        out_specs=[pl.BlockSpec((tm, D_MODEL), lambda i: (i, 0)),
                   pl.BlockSpec((tm, LANES), lambda i: (i, 0))],
        out_shape=[jax.ShapeDtypeStruct((N_TOK, D_MODEL), F32),
                   jax.ShapeDtypeStruct((N_TOK, LANES), F32)],
        compiler_params=_cparams(("arbitrary",), VMEM_LIMIT),
        name="wo_ln",
    )(*branches_p, x_p, *branches_s, x_s, wo, g, b, rw, rb)


def _route(logits):
    top_v, top_i = lax.top_k(logits, TOP_K)
    gates = jax.nn.softmax(top_v, axis=-1)
    na = N_TOK * TOP_K
    flat_e = top_i.reshape(-1).astype(jnp.int32)
    order = jnp.argsort(flat_e).astype(jnp.int32)
    inv_order = jnp.argsort(order).astype(jnp.int32)
    se = flat_e[order]
    eids = jnp.arange(N_EXP, dtype=jnp.int32)
    counts = jnp.sum((flat_e[:, None] == eids[None, :]).astype(jnp.int32), axis=0)
    padded = (counts + MOE_TB - 1) // MOE_TB * MOE_TB
    pend = jnp.cumsum(padded)
    pstart = pend - padded
    ustart = jnp.cumsum(counts) - counts
    dest = pstart[se] + jnp.arange(na, dtype=jnp.int32) - ustart[se]
    pos = dest[inv_order]
    blk_start = jnp.arange(MOE_NB, dtype=jnp.int32) * MOE_TB
    blk_e = jnp.minimum(jnp.sum((pend[None, :] <= blk_start[:, None]).astype(jnp.int32), axis=1),
                        N_EXP - 1)
    off = (blk_start - pstart[blk_e])[:, None] + jnp.arange(MOE_TB, dtype=jnp.int32)[None, :]
    src = jnp.clip(ustart[blk_e][:, None] + off, 0, na - 1)
    row_tok = jnp.where(off < counts[blk_e][:, None], order[src] // TOP_K, 0).reshape(-1)
    n_used = (pend[-1] // MOE_TB).astype(jnp.int32).reshape(1)
    has = counts > 0
    later = jnp.where((eids[None, :] > eids[:, None]) & has[None, :], eids[None, :], N_EXP)
    first_e = jnp.min(jnp.where(has, eids, N_EXP))
    nxt_of = jnp.min(later, axis=1)
    nxt_e = jnp.where(nxt_of < N_EXP, nxt_of, first_e)[blk_e].astype(jnp.int32)
    return gates, row_tok, pos, blk_e, nxt_e, n_used


def _row_copy(src_hbm, idx, dst, slot, sem):
    return pltpu.make_async_copy(src_hbm.at[pl.ds(idx, 1)], dst.at[pl.ds(slot, 1)], sem)


def _issue_rows(src_hbm, idx_ref, n, dst, sem):
    def body(r, carry):
        _row_copy(src_hbm, idx_ref[r], dst, r, sem).start()
        return carry

    lax.fori_loop(0, n, body, 0, unroll=8)


def _wait_rows(src_hbm, n, dst, sem):
    pltpu.make_async_copy(src_hbm.at[pl.ds(0, n)], dst, sem).wait()


def _gather_kernel(nused_ref, tok_ref, tok_next_ref, x_hbm, o_ref, buf, sem):
    i = pl.program_id(0)
    nu = nused_ref[0]
    slot = i % 2

    @pl.when(i == 0)
    def _():
        _issue_rows(x_hbm, tok_ref, MOE_TB, buf.at[0], sem.at[0])

    @pl.when(i + 1 < nu)
    def _():
        _issue_rows(x_hbm, tok_next_ref, MOE_TB, buf.at[1 - slot], sem.at[1 - slot])

    @pl.when(i < nu)
    def _():
        _wait_rows(x_hbm, MOE_TB, buf.at[slot], sem.at[slot])
        o_ref[...] = buf[slot].astype(BF16)

    @pl.when(i >= nu)
    def _():
        o_ref[...] = jnp.zeros_like(o_ref)


def _moe_gather(x1, row_tok, n_used):
    return pl.pallas_call(
        _gather_kernel,
        grid_spec=pltpu.PrefetchScalarGridSpec(
            num_scalar_prefetch=1,
            grid=(MOE_NB,),
            in_specs=[pl.BlockSpec((MOE_TB,), lambda i, nu: (i,), memory_space=pltpu.SMEM),
                      pl.BlockSpec((MOE_TB,), lambda i, nu: (jnp.minimum(i + 1, MOE_NB - 1),),
                                   memory_space=pltpu.SMEM),
                      pl.BlockSpec(memory_space=pl.ANY)],
            out_specs=pl.BlockSpec((MOE_TB, D_MODEL), lambda i, nu: (i, 0)),
            scratch_shapes=[pltpu.VMEM((2, MOE_TB, D_MODEL), F32), pltpu.SemaphoreType.DMA((2,))],
        ),
        out_shape=jax.ShapeDtypeStruct((MOE_P, D_MODEL), BF16),
        compiler_params=_cparams(("arbitrary",)),
        name="moe_gather",
    )(n_used, row_tok, row_tok, x1)


def _expert_changed(be_ref, i):
    return (i == 0) | (be_ref[i] != be_ref[jnp.maximum(i - 1, 0)])


def _stream_weight_tile(be_ref, nxt_ref, nused_ref, w_hbm, layer, col_starts, width, stage, dsts,
                        cnt_ref, sem):
    j = pl.program_id(0)
    i = pl.program_id(1)
    nu = nused_ref[0]

    def copies(e, jj, slot):
        return [pltpu.make_async_copy(
            w_hbm.at[layer, e, :, pl.ds(pl.multiple_of(c0 + jj * width, width), width)],
            stage.at[slot, n], sem.at[slot, n]) for n, c0 in enumerate(col_starts)]

    @pl.when((j == 0) & (i == 0))
    def _():
        cnt_ref[0] = 0
        for c in copies(be_ref[0], 0, 0):
            c.start()

    @pl.when((i < nu) & _expert_changed(be_ref, i))
    def _():
        k = cnt_ref[0]
        slot = k % 2
        for c in copies(be_ref[i], j, slot):
            c.wait()
        for n, dst in enumerate(dsts):
            dst[...] = stage[slot, n].astype(BF16)
        cnt_ref[0] = k + 1
        jn = jnp.where(be_ref[i] == be_ref[nu - 1], j + 1, j)

        @pl.when(jn < pl.num_programs(0))
        def _():
            for c in copies(nxt_ref[i], jn, 1 - slot):
                c.start()


def _expert_up_kernel(layer, be_ref, nxt_ref, nused_ref, x_ref, bg_ref, bu_ref, w_hbm, h_ref,
                      stage, wg_scr, wu_scr, cnt_ref, sem):
    i = pl.program_id(1)
    active = i < nused_ref[0]
    _stream_weight_tile(be_ref, nxt_ref, nused_ref, w_hbm, layer, (0, D_FF), MOE_TF, stage,
                        (wg_scr, wu_scr), cnt_ref, sem)

    @pl.when(active)
    def _():
        x = x_ref[...]
        gate = jnp.dot(x, wg_scr[...], preferred_element_type=F32) + bg_ref[0, 0]
        up = jnp.dot(x, wu_scr[...], preferred_element_type=F32) + bu_ref[0, 0]
        gate = jnp.minimum(gate, SWIGLU_LIMIT)
        up = jnp.clip(up, -SWIGLU_LIMIT, SWIGLU_LIMIT)
        h_ref[...] = ((up + 1.0) * (gate * jax.nn.sigmoid(SWIGLU_ALPHA * gate))).astype(h_ref.dtype)

    @pl.when(i >= nused_ref[0])
    def _():
        h_ref[...] = jnp.zeros_like(h_ref)


def _expert_up(xg, w_gu, b_gu, blk_e, nxt_e, n_used, layer):
    nf = D_FF // MOE_TF

    def used(i, nu):
        return jnp.minimum(i, nu[0] - 1)

    def bspec(col0):
        return pl.BlockSpec((1, 1, 1, MOE_TF),
                            lambda j, i, be, nx, nu: (layer, be[used(i, nu)], 0, col0 + j))

    return pl.pallas_call(
        functools.partial(_expert_up_kernel, layer),
        grid_spec=pltpu.PrefetchScalarGridSpec(
            num_scalar_prefetch=3,
            grid=(nf, MOE_NB),
            in_specs=[
                pl.BlockSpec((MOE_TB, D_MODEL), lambda j, i, be, nx, nu: (used(i, nu), 0)),
                bspec(0), bspec(nf), pl.BlockSpec(memory_space=pl.ANY),
            ],
            out_specs=pl.BlockSpec((MOE_TB, MOE_TF), lambda j, i, be, nx, nu: (i, j)),
            scratch_shapes=[pltpu.VMEM((2, 2, D_MODEL, MOE_TF), F32),
                            pltpu.VMEM((D_MODEL, MOE_TF), BF16), pltpu.VMEM((D_MODEL, MOE_TF), BF16),
                            pltpu.SMEM((1,), jnp.int32), pltpu.SemaphoreType.DMA((2, 2))],
        ),
        out_shape=jax.ShapeDtypeStruct((MOE_P, D_FF), BF16),
        compiler_params=_cparams(("arbitrary", "arbitrary"), VMEM_LIMIT),
        name="expert_up",
    )(blk_e, nxt_e, n_used, xg, b_gu, b_gu, w_gu)


def _expert_down_kernel(layer, be_ref, nxt_ref, nused_ref, h_ref, b_ref, w_hbm, y_ref,
                        stage, w_scr, cnt_ref, sem):
    i = pl.program_id(1)
    active = i < nused_ref[0]
    _stream_weight_tile(be_ref, nxt_ref, nused_ref, w_hbm, layer, (0,), MOE_TN, stage, (w_scr,),
                        cnt_ref, sem)

    @pl.when(active)
    def _():
        y_ref[...] = jnp.dot(h_ref[...], w_scr[...], preferred_element_type=F32) + b_ref[0, 0]

    @pl.when(i >= nused_ref[0])
    def _():
        y_ref[...] = jnp.zeros_like(y_ref)


def _expert_down(h, w_dn, b_dn, blk_e, nxt_e, n_used, layer):
    nn = D_MODEL // MOE_TN

    def used(i, nu):
        return jnp.minimum(i, nu[0] - 1)

    return pl.pallas_call(
        functools.partial(_expert_down_kernel, layer),
        grid_spec=pltpu.PrefetchScalarGridSpec(
            num_scalar_prefetch=3,
            grid=(nn, MOE_NB),
            in_specs=[
                pl.BlockSpec((MOE_TB, D_FF), lambda j, i, be, nx, nu: (used(i, nu), 0)),
                pl.BlockSpec((1, 1, 1, MOE_TN),
                             lambda j, i, be, nx, nu: (layer, be[used(i, nu)], 0, j)),
                pl.BlockSpec(memory_space=pl.ANY),
            ],
            out_specs=pl.BlockSpec((MOE_TB, MOE_TN), lambda j, i, be, nx, nu: (i, j)),
            scratch_shapes=[pltpu.VMEM((2, 1, D_FF, MOE_TN), F32), pltpu.VMEM((D_FF, MOE_TN), BF16),
                            pltpu.SMEM((1,), jnp.int32), pltpu.SemaphoreType.DMA((2, 1))],
        ),
        out_shape=jax.ShapeDtypeStruct((MOE_P, D_MODEL), F32),
        compiler_params=_cparams(("arbitrary", "arbitrary"), VMEM_LIMIT),
        name="expert_down",
    )(blk_e, nxt_e, n_used, h, b_dn, w_dn)


def _combine_kernel(pos_ref, pos_next_ref, gates_ref, x1_ref, y_hbm, g_ref, b_ref, x2_ref, x2b_ref,
                    buf, sem):
    i = pl.program_id(0)
    slot = i % 2
    n = CMB_T * TOP_K

    @pl.when(i == 0)
    def _():
        _issue_rows(y_hbm, pos_ref, n, buf.at[0], sem.at[0])

    @pl.when(i + 1 < pl.num_programs(0))
    def _():
        _issue_rows(y_hbm, pos_next_ref, n, buf.at[1 - slot], sem.at[1 - slot])

    _wait_rows(y_hbm, n, buf.at[slot], sem.at[slot])
    gates = gates_ref[...]
    y = jnp.zeros((CMB_T, D_MODEL), F32)
    for k in range(TOP_K):
        y = y + buf[slot, k * CMB_T:(k + 1) * CMB_T, :] * gates[:, k:k + 1]
    x2 = _layernorm(DN_ALPHA * x1_ref[...] + y, g_ref[...], b_ref[...])
    x2_ref[...] = x2
    x2b_ref[...] = x2.astype(BF16)


def _moe_combine(pos_kmajor, gates, x1, y, g, b):
    row = pl.BlockSpec((CMB_T, D_MODEL), lambda i: (i, 0))
    vec = pl.BlockSpec((1, D_MODEL), lambda i: (0, 0))
    steps = N_TOK // CMB_T
    return pl.pallas_call(
        _combine_kernel,
        grid=(steps,),
        in_specs=[pl.BlockSpec((CMB_T * TOP_K,), lambda i: (i,), memory_space=pltpu.SMEM),
                  pl.BlockSpec((CMB_T * TOP_K,), lambda i: (jnp.minimum(i + 1, steps - 1),),
                               memory_space=pltpu.SMEM),
                  pl.BlockSpec((CMB_T, TOP_K), lambda i: (i, 0)),
                  row, pl.BlockSpec(memory_space=pl.ANY), vec, vec],
        out_specs=[row, row],
        out_shape=[jax.ShapeDtypeStruct((N_TOK, D_MODEL), F32),
                   jax.ShapeDtypeStruct((N_TOK, D_MODEL), BF16)],
        scratch_shapes=[pltpu.VMEM((2, CMB_T * TOP_K, D_MODEL), F32),
                        pltpu.SemaphoreType.DMA((2,))],
        compiler_params=_cparams(("arbitrary",)),
        name="moe_combine",
    )(pos_kmajor, pos_kmajor, gates, x1, y, g, b)


def _permute_w_in(w):
    parts, off = [], 0
    for n in REF_IN_SIZES:
        parts.append((off, n))
        off += n
    return jnp.concatenate([w[:, parts[i][0]:parts[i][0] + parts[i][1]] for i in MY_ORDER],
                           axis=1).astype(BF16)


def kernel(x_prompt, x_sample, cache_conv, state_ret, cache_swa_k, cache_swa_v, cache_mem_k, cache_mem_v, mem_prompt, w_in, conv_w, ret_gn_w, attn_sinks, w_mem_kv, w_o, ln1_g, ln1_b, router_w, router_b, w_gate_up, b_gate_up, w_down, b_down, ln2_g, ln2_b):
    pos_p = jnp.arange(SEQ, dtype=jnp.int32)
    pos_s = PAST_LEN + jnp.arange(DEC_SEQ, dtype=jnp.int32)
    ret_tabs_p = _ret_tables(pos_p, R_CHUNK)
    ret_tabs_s = _ret_tables(pos_s, DEC_SEQ)
    swa_tabs_p = _swa_tables(pos_p)
    swa_tabs_s = _swa_tables(pos_s)
    kvw = A_KV * A_HD

    xp = x_prompt.reshape(N_PROMPT, D_MODEL)
    xs = x_sample.reshape(N_SAMPLE, D_MODEL)
    xb = jnp.concatenate([xp, xs], axis=0).astype(BF16)
    mem_b = mem_prompt.reshape(BATCH * N_MEM, D_MODEL).astype(BF16)
    zeros_e = jnp.zeros((256, D_MODEL), F32)
    b_gu = b_gate_up.reshape(DEPTH, N_EXP, 1, 2 * D_FF)
    b_dn = b_down.reshape(DEPTH, N_EXP, 1, D_MODEL)
    swa_k = cache_swa_k.reshape(DEPTH, DEC_BATCH, WINDOW, kvw)
    swa_v = cache_swa_v.reshape(DEPTH, DEC_BATCH, WINDOW, kvw)

    outs = {k: [] for k in ("p_conv", "p_ret", "p_k", "p_v", "p_mk", "p_mv",
                            "s_conv", "s_ret", "s_k", "s_v")}
    x2 = None
    for l in range(DEPTH):
        z = _matmul(xb, _permute_w_in(w_in[l]), 2304, 512)
        gn = ret_gn_w[l].reshape(1, D_MODEL)
        sinks = attn_sinks[l]

        ya_p, tail_p = _conv_branch(z, conv_w[l], zeros_e, zeros_e, 0, N_PROMPT, SEQ, False)
        yb_p, ret_p = _ret_prompt(z, ret_tabs_p, gn)
        yc_p, kr_p = _swa_prompt(z, sinks, swa_tabs_p)
        mkv = _matmul(mem_b, w_mem_kv[l].astype(BF16), 512, 512)
        mk_p = mkv[:, :D_MODEL].reshape(BATCH, N_MEM, D_MODEL)
        mv_p = mkv[:, D_MODEL:].reshape(BATCH, N_MEM, D_MODEL)
        ym_p = _mem_attend(z, mk_p, mv_p, 0, BATCH, SEQ, 512, "mem_prompt")

        ea = jnp.repeat(cache_conv[l][:, 0], DEC_SEQ, axis=0)
        eb = jnp.repeat(cache_conv[l][:, 1], DEC_SEQ, axis=0)
        ya_s, u_s = _conv_branch(z, conv_w[l], ea, eb, N_PROMPT, N_SAMPLE, DEC_SEQ, True)
        yb_s, ret_s = _ret_sample(z, ret_tabs_s, gn, state_ret, l)
        yc_s, nk_s, nv_s = _swa_sample(z, sinks, swa_tabs_s, swa_k, swa_v, l)
        ym_s = _mem_attend(z, cache_mem_k, cache_mem_v, N_PROMPT, DEC_BATCH, DEC_SEQ, DEC_SEQ,
                           "mem_sample", layer=l)

        wo_b = w_o[l].astype(BF16)
        g1 = ln1_g[l].reshape(1, D_MODEL)
        b1 = ln1_b[l].reshape(1, D_MODEL)
        rw = jnp.pad(router_w[l], ((0, 0), (0, LANES - N_EXP))).astype(BF16)
        rb = jnp.pad(router_b[l], (0, LANES - N_EXP)).reshape(1, LANES)
        x_p, x_s, xs_row0 = (xp, xs, 0) if l == 0 else (x2, x2, N_PROMPT)
        x1, lg = _wo_ln((ya_p, yb_p, yc_p, ym_p), x_p, (ya_s, yb_s, yc_s, ym_s), x_s, xs_row0,
                        wo_b, g1, b1, rw, rb)
        logits = lg[:, :N_EXP]

        gates, row_tok, pos, blk_e, nxt_e, n_used = _route(logits)
        xg = _moe_gather(x1, row_tok, n_used)
        h = _expert_up(xg, w_gate_up, b_gu, blk_e, nxt_e, n_used, l)
        y = _expert_down(h, w_down, b_dn, blk_e, nxt_e, n_used, l)
        pos_k = pos.reshape(N_TOK // CMB_T, CMB_T, TOP_K).transpose(0, 2, 1).reshape(-1)
        x2, xb = _moe_combine(pos_k, gates, x1, y, ln2_g[l].reshape(1, D_MODEL),
                              ln2_b[l].reshape(1, D_MODEL))

        tail_p = tail_p.reshape(BATCH, SEQ // 256, SUBLANES, D_MODEL)
        outs["p_conv"].append(tail_p[:, -1, -(CONV_W - 1):])
        outs["p_ret"].append(ret_p)
        outs["p_k"].append(kr_p.reshape(BATCH, SEQ, A_KV, A_HD)[:, -WINDOW:])
        outs["p_v"].append(z[:N_PROMPT, AV:AV + kvw].reshape(BATCH, SEQ, A_KV, A_HD)[:, -WINDOW:])
        outs["p_mk"].append(mk_p.reshape(BATCH, N_MEM, M_HEADS, M_HD))
        outs["p_mv"].append(mv_p.reshape(BATCH, N_MEM, M_HEADS, M_HD))
        outs["s_conv"].append(u_s.reshape(DEC_BATCH, DEC_SEQ, D_MODEL)[:, -(CONV_W - 1):])
        outs["s_ret"].append(ret_s)
        outs["s_k"].append(nk_s.reshape(DEC_BATCH, WINDOW, A_KV, A_HD))
        outs["s_v"].append(nv_s.reshape(DEC_BATCH, WINDOW, A_KV, A_HD))

    return (x2[:N_PROMPT].reshape(BATCH, SEQ, D_MODEL),
            x2[N_PROMPT:].reshape(DEC_BATCH, DEC_SEQ, D_MODEL),
            jnp.stack(outs["p_conv"]), jnp.stack(outs["p_ret"]), jnp.stack(outs["p_k"]),
            jnp.stack(outs["p_v"]), jnp.stack(outs["p_mk"]), jnp.stack(outs["p_mv"]),
            jnp.stack(outs["s_conv"]), jnp.stack(outs["s_ret"]), jnp.stack(outs["s_k"]),
            jnp.stack(outs["s_v"]))
```

```python
import functools

import jax
import jax.numpy as jnp
from jax import lax
from jax.experimental import pallas as pl
from jax.experimental.pallas import tpu as pltpu

F32 = jnp.float32
BF16 = jnp.bfloat16

D_MODEL = 2048
BATCH = 2
SEQ = 4096
DEPTH = 2
DEC_BATCH = 128
DEC_SEQ = 8
PAST_LEN = 8192
N_PROMPT = BATCH * SEQ
N_SAMPLE = DEC_BATCH * DEC_SEQ
N_TOK = N_PROMPT + N_SAMPLE

CONV_W = 3
R_HEADS = 8
R_DK = 128
R_DV = D_MODEL // R_HEADS
R_CHUNK = 128
R_THETA = 10000.0
A_HEADS = 32
A_KV = 4
A_HD = D_MODEL // A_HEADS
A_GROUP = A_HEADS // A_KV
WINDOW = 128
ROPE_DIMS = A_HD // 4
ROPE_THETA = 500000.0
N_MEM = 256
M_HEADS = 4
M_HD = D_MODEL // M_HEADS
N_EXP = 32
TOP_K = 4
D_FF = D_MODEL
SWIGLU_LIMIT = 7.0
SWIGLU_ALPHA = 1.702
LN_EPS = 1e-5
GN_EPS = 1e-6
DN_ALPHA = (2 * DEPTH) ** 0.25

REF_IN_SIZES = (D_MODEL, D_MODEL, D_MODEL, R_HEADS * R_DK, R_HEADS * R_DK, D_MODEL, D_MODEL,
                D_MODEL, A_KV * A_HD, A_KV * A_HD, D_MODEL, 4 * D_MODEL)
D_IN = sum(REF_IN_SIZES)
MY_ORDER = (11, 0, 1, 2, 5, 6, 7, 10, 3, 4, 8, 9)
GL, HC, BC, CC, RV, RG, AQ, MQ, RQ, RK, AK, AV = (
    0, 8192, 10240, 12288, 14336, 16384, 18432, 20480, 22528, 23552, 24576, 24832)

LANES = 128
SUBLANES = 8
VMEM_LIMIT = 56 * 1024 * 1024

MOE_TB = 256
MOE_NB = N_TOK * TOP_K // MOE_TB + N_EXP
MOE_P = MOE_NB * MOE_TB
MOE_TF = 1024
MOE_TN = 2048
CMB_T = 64


def _cparams(sem, vmem=None):
    return pltpu.CompilerParams(dimension_semantics=sem, vmem_limit_bytes=vmem)


def _mm_kernel(x_ref, w_ref, o_ref):
    o_ref[...] = jnp.dot(x_ref[...], w_ref[...], preferred_element_type=F32).astype(o_ref.dtype)


def _matmul(x, w, tm, tn, out_dtype=F32):
    m, k = x.shape
    n = w.shape[1]
    return pl.pallas_call(
        _mm_kernel,
        grid=(m // tm, n // tn),
        in_specs=[pl.BlockSpec((tm, k), lambda i, j: (i, 0)),
                  pl.BlockSpec((k, tn), lambda i, j: (0, j))],
        out_specs=pl.BlockSpec((tm, tn), lambda i, j: (i, j)),
        out_shape=jax.ShapeDtypeStruct((m, n), out_dtype),
        compiler_params=_cparams(("parallel", "parallel"), VMEM_LIMIT),
        name="matmul",
    )(x, w)


def _conv_kernel(seq_len, full_u, hc_ref, cc_ref, bc_ref, g_ref, hh_ref, ch_ref, ea_ref, eb_ref,
                 w_ref, ya_ref, tail_ref):
    i = pl.program_id(0)
    tm = hc_ref.shape[0]
    u = cc_ref[...] * hc_ref[...]
    ue = jnp.concatenate([ch_ref[...] * hh_ref[...], u], axis=0)
    p1 = pltpu.roll(ue, 1, 0)[SUBLANES:]
    p2 = pltpu.roll(ue, 2, 0)[SUBLANES:]
    t = (i * tm + lax.broadcasted_iota(jnp.int32, (tm, 1), 0)) % seq_len
    ea = ea_ref[...]
    eb = eb_ref[...]
    p1 = jnp.where(t == 0, eb, p1)
    p2 = jnp.where(t == 0, ea, jnp.where(t == 1, eb, p2))
    w = w_ref[...]
    y = p2 * w[0:1] + p1 * w[1:2] + u * w[2:3]
    ya_ref[...] = jax.nn.sigmoid(g_ref[...]) * (bc_ref[...] * y)
    tail_ref[...] = u if full_u else u[tm - SUBLANES:]


def _conv_branch(z, conv_w, ea, eb, row0, rows, seq_len, sample):
    tm, tc = 256, 512
    r0 = row0 // tm
    nj = D_MODEL // tc

    def zspec(col):
        return pl.BlockSpec((tm, tc), lambda i, j: (r0 + i, col // tc + j))

    def halo(col):
        return pl.BlockSpec(
            (SUBLANES, tc),
            lambda i, j: (jnp.maximum((r0 + i) * (tm // SUBLANES) - 1, 0), col // tc + j))

    if sample:
        espec = pl.BlockSpec((tm, tc), lambda i, j: (i, j))
        tail_spec = pl.BlockSpec((tm, tc), lambda i, j: (i, j))
        tail_shape = jax.ShapeDtypeStruct((rows, D_MODEL), F32)
    else:
        espec = pl.BlockSpec((tm, tc), lambda i, j: (0, j))
        tail_spec = pl.BlockSpec((SUBLANES, tc), lambda i, j: (i, j))
        tail_shape = jax.ShapeDtypeStruct((rows // tm * SUBLANES, D_MODEL), F32)
    return pl.pallas_call(
        functools.partial(_conv_kernel, seq_len, sample),
        grid=(rows // tm, nj),
        in_specs=[zspec(HC), zspec(CC), zspec(BC), zspec(GL), halo(HC), halo(CC), espec, espec,
                  pl.BlockSpec((CONV_W, tc), lambda i, j: (0, j))],
        out_specs=[pl.BlockSpec((tm, tc), lambda i, j: (i, j)), tail_spec],
        out_shape=[jax.ShapeDtypeStruct((rows, D_MODEL), F32), tail_shape],
        compiler_params=_cparams(("parallel", "parallel"), VMEM_LIMIT),
        name="conv_sample" if sample else "conv_prompt",
    )(z, z, z, z, z, z, ea, eb, conv_w)


def _rope_full(x, cos, sin):
    return x * cos + pltpu.roll(x, R_DK // 2, 1) * sin


def _bf16_exact(x):
    return x.astype(BF16).astype(F32)


def _group_norm_gate(o, gn, rg, g):
    mu = jnp.mean(o, axis=-1, keepdims=True)
    d = o - mu
    var = jnp.mean(d * d, axis=-1, keepdims=True)
    yn = d * lax.rsqrt(var + GN_EPS) * gn
    return jax.nn.sigmoid(g) * ((rg * jax.nn.sigmoid(rg)) * yn)


def _ret_prompt_kernel(q_ref, k_ref, v_ref, rg_ref, g_ref, cos_ref, sin_ref, dmask_ref, qdec_ref,
                       kdec_ref, cdec_ref, gn_ref, y_ref, sfin_ref, s_scr):
    c = pl.program_id(1)

    @pl.when(c == 0)
    def _():
        s_scr[...] = jnp.zeros_like(s_scr)

    cos = cos_ref[...]
    sin = sin_ref[...]
    for h in range(R_HEADS):
        qs = slice(h * R_DK, (h + 1) * R_DK)
        vs = slice(h * R_DV, (h + 1) * R_DV)
        q = _rope_full(q_ref[:, qs], cos, sin)
        k = _rope_full(k_ref[:, qs], cos, sin) * (R_DK ** -0.5)
        vb = v_ref[:, vs].astype(BF16)
        s = s_scr[h]
        att = lax.dot_general(q.astype(BF16), k.astype(BF16), (((1,), (1,)), ((), ())),
                              preferred_element_type=F32) * dmask_ref[h]
        o = (jnp.dot(att.astype(BF16), vb, preferred_element_type=F32)
             + jnp.dot((q * qdec_ref[h]).astype(BF16), s.astype(BF16), preferred_element_type=F32))
        kd = (k * kdec_ref[h]).astype(BF16)
        s_new = cdec_ref[h] * s + lax.dot_general(kd, vb, (((0,), (0,)), ((), ())),
                                                  preferred_element_type=F32)
        s_scr[h] = s_new
        sfin_ref[0, h] = s_new
        y_ref[:, vs] = _group_norm_gate(o, gn_ref[:, vs], rg_ref[:, vs], g_ref[:, vs])


def _ret_prompt(z, tabs, gn_w):
    cos, sin, dmask, qdec, kdec, cdec = tabs
    nc = SEQ // R_CHUNK

    def zspec(col, width):
        return pl.BlockSpec((R_CHUNK, width), lambda b, c: (b * nc + c, col // width))

    def full(a):
        return pl.BlockSpec(a.shape, lambda b, c: (0,) * a.ndim)

    tspec = pl.BlockSpec((R_CHUNK, R_DK), lambda b, c: (c, 0))
    return pl.pallas_call(
        _ret_prompt_kernel,
        grid=(BATCH, nc),
        in_specs=[zspec(RQ, R_HEADS * R_DK), zspec(RK, R_HEADS * R_DK), zspec(RV, D_MODEL),
                  zspec(RG, D_MODEL), zspec(GL + D_MODEL, D_MODEL), tspec, tspec,
                  full(dmask), full(qdec), full(kdec), full(cdec), full(gn_w)],
        out_specs=[pl.BlockSpec((R_CHUNK, D_MODEL), lambda b, c: (b * nc + c, 0)),
                   pl.BlockSpec((1, R_HEADS, R_DK, R_DV), lambda b, c: (b, 0, 0, 0))],
        out_shape=[jax.ShapeDtypeStruct((N_PROMPT, D_MODEL), F32),
                   jax.ShapeDtypeStruct((BATCH, R_HEADS, R_DK, R_DV), F32)],
        scratch_shapes=[pltpu.VMEM((R_HEADS, R_DK, R_DV), F32)],
        compiler_params=_cparams(("parallel", "arbitrary"), VMEM_LIMIT),
        name="retention_prompt",
    )(z, z, z, z, z, cos, sin, dmask, qdec, kdec, cdec, gn_w)


def _ret_sample_kernel(q_ref, k_ref, v_ref, rg_ref, g_ref, cos_ref, sin_ref, dmask_ref, qdec_ref,
                       kdec_ref, cdec_ref, gn_ref, s_ref, y_ref, snew_ref):
    cos = cos_ref[...]
    sin = sin_ref[...]
    for h in range(R_HEADS):
        qs = slice(h * R_DK, (h + 1) * R_DK)
        vs = slice(h * R_DV, (h + 1) * R_DV)
        q = _rope_full(q_ref[:, qs], cos, sin)
        k = _rope_full(k_ref[:, qs], cos, sin) * (R_DK ** -0.5)
        v = _bf16_exact(v_ref[:, vs])
        s = s_ref[0, 0, h]
        att = lax.dot_general(_bf16_exact(q), _bf16_exact(k), (((1,), (1,)), ((), ())),
                              preferred_element_type=F32) * dmask_ref[h]
        o = (jnp.dot(_bf16_exact(att), v, preferred_element_type=F32)
             + jnp.dot((q * qdec_ref[h]).astype(BF16), s.astype(BF16), preferred_element_type=F32))
        kd = _bf16_exact(k * kdec_ref[h])
        snew_ref[0, h] = cdec_ref[h] * s + lax.dot_general(
            kd, v, (((0,), (0,)), ((), ())), preferred_element_type=F32)
        y_ref[:, vs] = _group_norm_gate(o, gn_ref[:, vs], rg_ref[:, vs], g_ref[:, vs])


def _ret_sample(z, tabs, gn_w, state, layer):
    cos, sin, dmask, qdec, kdec, cdec = tabs
    r0 = N_PROMPT // DEC_SEQ

    def zspec(col, width):
        return pl.BlockSpec((DEC_SEQ, width), lambda b: (r0 + b, col // width))

    def full(a):
        return pl.BlockSpec(a.shape, lambda b: (0,) * a.ndim)

    sspec = pl.BlockSpec((1, R_HEADS, R_DK, R_DV), lambda b: (b, 0, 0, 0))
    sin_spec = pl.BlockSpec((1, 1, R_HEADS, R_DK, R_DV), lambda b: (layer, b, 0, 0, 0))
    return pl.pallas_call(
        _ret_sample_kernel,
        grid=(DEC_BATCH,),
        in_specs=[zspec(RQ, R_HEADS * R_DK), zspec(RK, R_HEADS * R_DK), zspec(RV, D_MODEL),
                  zspec(RG, D_MODEL), zspec(GL + D_MODEL, D_MODEL),
                  full(cos), full(sin), full(dmask), full(qdec), full(kdec), full(cdec), full(gn_w),
                  sin_spec],
        out_specs=[pl.BlockSpec((DEC_SEQ, D_MODEL), lambda b: (b, 0)), sspec],
        out_shape=[jax.ShapeDtypeStruct((N_SAMPLE, D_MODEL), F32),
                   jax.ShapeDtypeStruct(state.shape[1:], F32)],
        compiler_params=_cparams(("parallel",)),
        name="retention_sample",
    )(z, z, z, z, z, cos, sin, dmask, qdec, kdec, cdec, gn_w, state)


def _ret_tables(pos, chunk):
    half = R_DK // 2
    inv = R_THETA ** (-jnp.arange(half, dtype=F32) / half)
    ang = pos.astype(F32)[:, None] * inv[None, :]
    cos = jnp.concatenate([jnp.cos(ang), jnp.cos(ang)], axis=-1)
    sin = jnp.concatenate([-jnp.sin(ang), jnp.sin(ang)], axis=-1)
    lg = jnp.log(1.0 - 2.0 ** (-5.0 - jnp.arange(R_HEADS, dtype=F32)))
    i = jnp.arange(chunk, dtype=F32)
    diff = i[:, None] - i[None, :]
    dmask = jnp.where(diff >= 0, jnp.exp(lg[:, None, None] * jnp.maximum(diff, 0.0)), 0.0)
    qdec = jnp.broadcast_to(jnp.exp(lg[:, None] * (i + 1.0))[:, :, None], (R_HEADS, chunk, R_DK))
    kdec = jnp.broadcast_to(jnp.exp(lg[:, None] * (chunk - 1.0 - i))[:, :, None],
                            (R_HEADS, chunk, R_DK))
    cdec = jnp.broadcast_to(jnp.exp(lg * chunk)[:, None, None], (R_HEADS, 1, R_DV))
    return cos, sin, dmask.astype(F32), qdec.astype(F32), kdec.astype(F32), cdec.astype(F32)


def _swa_tables(pos):
    half = ROPE_DIMS // 2
    inv = ROPE_THETA ** (-jnp.arange(half, dtype=F32) / half)
    ang = pos.astype(F32)[:, None] * inv[None, :]
    cos, sin = jnp.cos(ang), jnp.sin(ang)
    t = pos.shape[0]
    ones = jnp.ones((t, A_HD - ROPE_DIMS), F32)
    zeros = jnp.zeros((t, A_HD - ROPE_DIMS), F32)
    zh = jnp.zeros((t, half), F32)
    c = jnp.concatenate([cos, cos, ones], axis=-1)
    s_lo = jnp.concatenate([-sin, zh, zeros], axis=-1)
    s_hi = jnp.concatenate([zh, sin, zeros], axis=-1)
    rep = LANES // A_HD
    return jnp.tile(c, (1, rep)), jnp.tile(s_lo, (1, rep)), jnp.tile(s_hi, (1, rep))


def _rope_partial(x, c, s_lo, s_hi):
    half = ROPE_DIMS // 2
    return x * c + pltpu.roll(x, LANES - half, 1) * s_lo + pltpu.roll(x, half, 1) * s_hi


def _dup_head(chunk, odd):
    lane = lax.broadcasted_iota(jnp.int32, chunk.shape, 1)
    sw = pltpu.roll(chunk, A_HD, 1)
    if odd:
        return jnp.where(lane < A_HD, sw, chunk)
    return jnp.where(lane < A_HD, chunk, sw)


def _swa_core(sink_ref, q_ref, g_ref, y_ref, kp, kc, vp, vc, tabs_q, has_prev, tq, chunks_per_dot):
    cq, slq, shq = tabs_q
    lane_q = lax.broadcasted_iota(jnp.int32, (tq, LANES), 1)
    n_chunks = A_GROUP * A_HD // LANES
    for h in range(A_KV):
        kchunk = slice((h // 2) * LANES, (h // 2 + 1) * LANES)
        kkp = _dup_head(kp[:, kchunk], h % 2).astype(BF16)
        kkc = _dup_head(kc[:, kchunk], h % 2).astype(BF16)
        vvp = _dup_head(vp[:, kchunk], h % 2).astype(BF16)
        vvc = _dup_head(vc[:, kchunk], h % 2).astype(BF16)
        for c0 in range(0, n_chunks, chunks_per_dot):
            pieces, sinks = [], []
            for ci in range(c0, c0 + chunks_per_dot):
                col = h * A_GROUP * A_HD + ci * LANES
                qc = _rope_partial(q_ref[:, col:col + LANES], cq, slq, shq) * (A_HD ** -0.5)
                for half in range(2):
                    keep = (lane_q < A_HD) if half == 0 else (lane_q >= A_HD)
                    pieces.append(jnp.where(keep, qc, 0.0).astype(BF16))
                    head = h * A_GROUP + 2 * ci + half
                    sinks.append(jnp.full((tq, 1), sink_ref[head], F32))
            qs = jnp.concatenate(pieces, axis=0)
            sink = jnp.concatenate(sinks, axis=0)
            r = qs.shape[0]
            nt = (((1,), (1,)), ((), ()))
            sp = lax.dot_general(qs, kkp, nt, preferred_element_type=F32)
            sc = lax.dot_general(qs, kkc, nt, preferred_element_type=F32)
            t = lax.broadcasted_iota(jnp.int32, (r, WINDOW), 0) % tq
            j = lax.broadcasted_iota(jnp.int32, (r, WINDOW), 1)
            sp = jnp.where((j >= t) & has_prev, sp, -jnp.inf)
            sc = jnp.where(j <= t, sc, -jnp.inf)
            m = jnp.maximum(jnp.max(jnp.maximum(sp, sc), axis=-1, keepdims=True), sink)
            pp = jnp.exp(sp - m)
            pc = jnp.exp(sc - m)
            den = jnp.sum(pp + pc, axis=-1, keepdims=True) + jnp.exp(sink - m)
            inv = 1.0 / den
            o = (jnp.dot((pp * inv).astype(BF16), vvp, preferred_element_type=F32)
                 + jnp.dot((pc * inv).astype(BF16), vvc, preferred_element_type=F32))
            for n, ci in enumerate(range(c0, c0 + chunks_per_dot)):
                col = h * A_GROUP * A_HD + ci * LANES
                oa = o[(2 * n) * tq:(2 * n + 1) * tq]
                ob = o[(2 * n + 1) * tq:(2 * n + 2) * tq]
                oc = jnp.where(lane_q < A_HD, oa, ob)
                y_ref[:, col:col + LANES] = jax.nn.sigmoid(g_ref[:, col:col + LANES]) * oc


def _rope_kv(k_ref, ck, slk, shk):
    parts = [_rope_partial(k_ref[:, c * LANES:(c + 1) * LANES], ck, slk, shk)
             for c in range(A_KV * A_HD // LANES)]
    return jnp.concatenate(parts, axis=-1)


def _swa_prompt_kernel(sink_ref, q_ref, k_ref, v_ref, g_ref, c_ref, sl_ref, sh_ref,
                       y_ref, kr_ref, kp_scr, vp_scr):
    n = pl.program_id(1)

    @pl.when(n == 0)
    def _():
        kp_scr[...] = jnp.zeros_like(kp_scr)
        vp_scr[...] = jnp.zeros_like(vp_scr)

    tabs = (c_ref[...], sl_ref[...], sh_ref[...])
    kc = _rope_kv(k_ref, *tabs)
    vc = v_ref[...]
    kr_ref[...] = kc
    _swa_core(sink_ref, q_ref, g_ref, y_ref, kp_scr[...], kc, vp_scr[...], vc, tabs, n > 0,
              WINDOW, A_GROUP * A_HD // LANES)
    kp_scr[...] = kc
    vp_scr[...] = vc


def _swa_prompt(z, sinks, tabs):
    nb = SEQ // WINDOW
    kvw = A_KV * A_HD

    def zspec(col, width):
        return pl.BlockSpec((WINDOW, width), lambda b, n: (b * nb + n, col // width))

    tspec = pl.BlockSpec((WINDOW, LANES), lambda b, n: (n, 0))
    return pl.pallas_call(
        _swa_prompt_kernel,
        grid=(BATCH, nb),
        in_specs=[pl.BlockSpec(memory_space=pltpu.SMEM),
                  zspec(AQ, D_MODEL), zspec(AK, kvw), zspec(AV, kvw), zspec(GL + 2 * D_MODEL, D_MODEL),
                  tspec, tspec, tspec],
        out_specs=[pl.BlockSpec((WINDOW, D_MODEL), lambda b, n: (b * nb + n, 0)),
                   pl.BlockSpec((WINDOW, kvw), lambda b, n: (b * nb + n, 0))],
        out_shape=[jax.ShapeDtypeStruct((N_PROMPT, D_MODEL), F32),
                   jax.ShapeDtypeStruct((N_PROMPT, kvw), F32)],
        scratch_shapes=[pltpu.VMEM((WINDOW, kvw), F32), pltpu.VMEM((WINDOW, kvw), F32)],
        compiler_params=_cparams(("parallel", "arbitrary"), VMEM_LIMIT),
        name="swa_prompt",
    )(sinks, z, z, z, z, *tabs)


def _swa_sample_kernel(sink_ref, q_ref, k_ref, v_ref, g_ref, c_ref, sl_ref, sh_ref, ck_ref, cv_ref,
                       y_ref, nk_ref, nv_ref):
    tabs = (c_ref[...], sl_ref[...], sh_ref[...])
    kn = _rope_kv(k_ref, *tabs)
    vn = v_ref[...]
    pad = jnp.zeros((WINDOW - DEC_SEQ, A_KV * A_HD), F32)
    kp = ck_ref[0, 0]
    vp = cv_ref[0, 0]
    _swa_core(sink_ref, q_ref, g_ref, y_ref, kp, jnp.concatenate([kn, pad], axis=0),
              vp, jnp.concatenate([vn, pad], axis=0), tabs, True, DEC_SEQ,
              A_GROUP * A_HD // LANES)
    nk_ref[0] = jnp.concatenate([kp[DEC_SEQ:], kn], axis=0)
    nv_ref[0] = jnp.concatenate([vp[DEC_SEQ:], vn], axis=0)


def _swa_sample(z, sinks, tabs, cache_k, cache_v, layer):
    kvw = A_KV * A_HD
    r0 = N_PROMPT // DEC_SEQ

    def zspec(col, width):
        return pl.BlockSpec((DEC_SEQ, width), lambda b: (r0 + b, col // width))

    tspec = pl.BlockSpec((DEC_SEQ, LANES), lambda b: (0, 0))
    cspec = pl.BlockSpec((1, WINDOW, kvw), lambda b: (b, 0, 0))
    cin_spec = pl.BlockSpec((1, 1, WINDOW, kvw), lambda b: (layer, b, 0, 0))
    return pl.pallas_call(
        _swa_sample_kernel,
        grid=(DEC_BATCH,),
        in_specs=[pl.BlockSpec(memory_space=pltpu.SMEM),
                  zspec(AQ, D_MODEL), zspec(AK, kvw), zspec(AV, kvw), zspec(GL + 2 * D_MODEL, D_MODEL),
                  tspec, tspec, tspec, cin_spec, cin_spec],
        out_specs=[pl.BlockSpec((DEC_SEQ, D_MODEL), lambda b: (b, 0)), cspec, cspec],
        out_shape=[jax.ShapeDtypeStruct((N_SAMPLE, D_MODEL), F32),
                   jax.ShapeDtypeStruct((DEC_BATCH, WINDOW, kvw), F32),
                   jax.ShapeDtypeStruct((DEC_BATCH, WINDOW, kvw), F32)],
        compiler_params=_cparams(("parallel",)),
        name="swa_sample",
    )(sinks, z, z, z, z, *tabs, cache_k, cache_v)


def _mem_kernel(q_ref, k_ref, v_ref, g_ref, y_ref):
    for h in range(M_HEADS):
        hs = slice(h * M_HD, (h + 1) * M_HD)
        q = q_ref[:, hs].astype(BF16)
        k = k_ref[0, :, hs].astype(BF16)
        v = v_ref[0, :, hs].astype(BF16)
        s = lax.dot_general(q, k, (((1,), (1,)), ((), ())), preferred_element_type=F32) * (M_HD ** -0.5)
        m = jnp.max(s, axis=-1, keepdims=True)
        p = jnp.exp(s - m)
        p = p / jnp.sum(p, axis=-1, keepdims=True)
        o = jnp.dot(p.astype(BF16), v, preferred_element_type=F32)
        y_ref[:, hs] = jax.nn.sigmoid(g_ref[:, hs]) * o


def _mem_cache_kernel(q_ref, k_ref, v_ref, g_ref, y_ref):
    k2 = k_ref.reshape(N_MEM * M_HEADS, M_HD)
    v2 = v_ref.reshape(N_MEM * M_HEADS, M_HD)
    tq = q_ref.shape[0]
    qs = jnp.concatenate([q_ref[:, h * M_HD:(h + 1) * M_HD] for h in range(M_HEADS)],
                         axis=0).astype(BF16)
    s = lax.dot_general(qs, k2[...].astype(BF16), (((1,), (1,)), ((), ())),
                        preferred_element_type=F32) * (M_HD ** -0.5)
    row_h = lax.broadcasted_iota(jnp.int32, s.shape, 0) // tq
    col_h = lax.broadcasted_iota(jnp.int32, s.shape, 1) % M_HEADS
    s = jnp.where(row_h == col_h, s, -jnp.inf)
    m = jnp.max(s, axis=-1, keepdims=True)
    p = jnp.exp(s - m)
    p = p / jnp.sum(p, axis=-1, keepdims=True)
    o = jnp.dot(p.astype(BF16), v2[...].astype(BF16), preferred_element_type=F32)
    for h in range(M_HEADS):
        hs = slice(h * M_HD, (h + 1) * M_HD)
        y_ref[:, hs] = jax.nn.sigmoid(g_ref[:, hs]) * o[h * tq:(h + 1) * tq]


def _mem_attend(z, mk, mv, row0, nb, rows_per_b, tq, name, layer=None):
    nq = rows_per_b // tq
    r0 = row0 // tq

    def zspec(col):
        return pl.BlockSpec((tq, D_MODEL), lambda b, i: (r0 + b * nq + i, col // D_MODEL))

    if layer is None:
        kvspec = pl.BlockSpec((1, N_MEM, D_MODEL), lambda b, i: (b, 0, 0))
    else:
        kvspec = pl.BlockSpec((None, None, N_MEM, M_HEADS, M_HD),
                              lambda b, i: (layer, b, 0, 0, 0))
    return pl.pallas_call(
        _mem_kernel if layer is None else _mem_cache_kernel,
        grid=(nb, nq),
        in_specs=[zspec(MQ), kvspec, kvspec, zspec(GL + 3 * D_MODEL)],
        out_specs=pl.BlockSpec((tq, D_MODEL), lambda b, i: (b * nq + i, 0)),
        out_shape=jax.ShapeDtypeStruct((nb * rows_per_b, D_MODEL), F32),
        compiler_params=_cparams(("parallel", "parallel"), VMEM_LIMIT),
        name=name,
    )(z, mk, mv, z)


def _layernorm(r, g, b):
    mu = jnp.mean(r, axis=-1, keepdims=True)
    d = r - mu
    var = jnp.mean(d * d, axis=-1, keepdims=True)
    return d * lax.rsqrt(var + LN_EPS) * g + b


def _wo_ln_kernel(ya_ref, yb_ref, yc_ref, ym_ref, x_ref, wo_ref, g_ref, b_ref, rw_ref, rb_ref,
                  x1_ref, lg_ref):
    merged = ya_ref[...] + yb_ref[...] + yc_ref[...] + ym_ref[...]
    r = DN_ALPHA * x_ref[...] + jnp.dot(merged.astype(BF16), wo_ref[...], preferred_element_type=F32)
    x1 = _layernorm(r, g_ref[...], b_ref[...])
    x1_ref[...] = x1
    lg_ref[...] = jnp.dot(x1.astype(BF16), rw_ref[...], preferred_element_type=F32) + rb_ref[...]


def _wo_ln_both_kernel(n_prompt_tiles, *refs):
    i = pl.program_id(0)
    prompt, sample, rest = refs[:5], refs[5:10], refs[10:]

    @pl.when(i < n_prompt_tiles)
    def _():
        _wo_ln_kernel(*prompt, *rest)

    @pl.when(i >= n_prompt_tiles)
    def _():
        _wo_ln_kernel(*sample, *rest)


def _wo_ln(branches_p, x_p, branches_s, x_s, xs_row0, wo, g, b, rw, rb):
    tm = 128
    npt = N_PROMPT // tm
    xs0 = xs_row0 // tm
    p_row = pl.BlockSpec((tm, D_MODEL), lambda i: (jnp.minimum(i, npt - 1), 0))
    s_row = pl.BlockSpec((tm, D_MODEL), lambda i: (jnp.maximum(i - npt, 0), 0))
    sx_row = pl.BlockSpec((tm, D_MODEL), lambda i: (xs0 + jnp.maximum(i - npt, 0), 0))

    def full(a):
        return pl.BlockSpec(a.shape, lambda i: (0,) * a.ndim)

    return pl.pallas_call(
        functools.partial(_wo_ln_both_kernel, npt),
        grid=(N_TOK // tm,),
        in_specs=[p_row] * 5 + [s_row] * 4 + [sx_row]
        + [full(wo), full(g), full(b), full(rw), full(rb)],
        out_specs=[pl.BlockSpec((tm, D_MODEL), lambda i: (i, 0)),
                   pl.BlockSpec((tm, LANES), lambda i: (i, 0))],
        out_shape=[jax.ShapeDtypeStruct((N_TOK, D_MODEL), F32),
                   jax.ShapeDtypeStruct((N_TOK, LANES), F32)],
        compiler_params=_cparams(("arbitrary",), VMEM_LIMIT),
        name="wo_ln",
    )(*branches_p, x_p, *branches_s, x_s, wo, g, b, rw, rb)


def _route(logits):
    top_v, top_i = lax.top_k(logits, TOP_K)
    gates = jax.nn.softmax(top_v, axis=-1)
    na = N_TOK * TOP_K
    flat_e = top_i.reshape(-1).astype(jnp.int32)
    order = jnp.argsort(flat_e).astype(jnp.int32)
    inv_order = jnp.argsort(order).astype(jnp.int32)
    se = flat_e[order]
    eids = jnp.arange(N_EXP, dtype=jnp.int32)
    counts = jnp.sum((flat_e[:, None] == eids[None, :]).astype(jnp.int32), axis=0)
    padded = (counts + MOE_TB - 1) // MOE_TB * MOE_TB
    pend = jnp.cumsum(padded)
    pstart = pend - padded
    ustart = jnp.cumsum(counts) - counts
    dest = pstart[se] + jnp.arange(na, dtype=jnp.int32) - ustart[se]
    pos = dest[inv_order]
    blk_start = jnp.arange(MOE_NB, dtype=jnp.int32) * MOE_TB
    blk_e = jnp.minimum(jnp.sum((pend[None, :] <= blk_start[:, None]).astype(jnp.int32), axis=1),
                        N_EXP - 1)
    off = (blk_start - pstart[blk_e])[:, None] + jnp.arange(MOE_TB, dtype=jnp.int32)[None, :]
    src = jnp.clip(ustart[blk_e][:, None] + off, 0, na - 1)
    row_tok = jnp.where(off < counts[blk_e][:, None], order[src] // TOP_K, 0).reshape(-1)
    n_used = (pend[-1] // MOE_TB).astype(jnp.int32).reshape(1)
    has = counts > 0
    later = jnp.where((eids[None, :] > eids[:, None]) & has[None, :], eids[None, :], N_EXP)
    first_e = jnp.min(jnp.where(has, eids, N_EXP))
    nxt_of = jnp.min(later, axis=1)
    nxt_e = jnp.where(nxt_of < N_EXP, nxt_of, first_e)[blk_e].astype(jnp.int32)
    return gates, row_tok, pos, blk_e, nxt_e, n_used


def _row_copy(src_hbm, idx, dst, slot, sem):
    return pltpu.make_async_copy(src_hbm.at[pl.ds(idx, 1)], dst.at[pl.ds(slot, 1)], sem)


ROW_DMA_GROUP = 8


def _issue_rows(src_hbm, idx_ref, n, dst, sem):
    def body(g, carry):
        base = g * ROW_DMA_GROUP
        for k in range(ROW_DMA_GROUP):
            _row_copy(src_hbm, idx_ref[base + k], dst, base + k, sem).start(priority=k % 2)
        return carry

    lax.fori_loop(0, n // ROW_DMA_GROUP, body, 0)


def _wait_rows(src_hbm, n, dst, sem):
    pltpu.make_async_copy(src_hbm.at[pl.ds(0, n)], dst, sem).wait()


def _gather_kernel(nused_ref, tok_ref, tok_next_ref, x_hbm, o_ref, buf, sem):
    i = pl.program_id(0)
    nu = nused_ref[0]
    slot = i % 2

    @pl.when(i == 0)
    def _():
        _issue_rows(x_hbm, tok_ref, MOE_TB, buf.at[0], sem.at[0])

    @pl.when(i + 1 < nu)
    def _():
        _issue_rows(x_hbm, tok_next_ref, MOE_TB, buf.at[1 - slot], sem.at[1 - slot])

    @pl.when(i < nu)
    def _():
        _wait_rows(x_hbm, MOE_TB, buf.at[slot], sem.at[slot])
        o_ref[...] = buf[slot].astype(BF16)

    @pl.when(i >= nu)
    def _():
        o_ref[...] = jnp.zeros_like(o_ref)


def _moe_gather(x1, row_tok, n_used):
    return pl.pallas_call(
        _gather_kernel,
        grid_spec=pltpu.PrefetchScalarGridSpec(
            num_scalar_prefetch=1,
            grid=(MOE_NB,),
            in_specs=[pl.BlockSpec((MOE_TB,), lambda i, nu: (i,), memory_space=pltpu.SMEM),
                      pl.BlockSpec((MOE_TB,), lambda i, nu: (jnp.minimum(i + 1, MOE_NB - 1),),
                                   memory_space=pltpu.SMEM),
                      pl.BlockSpec(memory_space=pl.ANY)],
            out_specs=pl.BlockSpec((MOE_TB, D_MODEL), lambda i, nu: (i, 0)),
            scratch_shapes=[pltpu.VMEM((2, MOE_TB, D_MODEL), F32), pltpu.SemaphoreType.DMA((2,))],
        ),
        out_shape=jax.ShapeDtypeStruct((MOE_P, D_MODEL), BF16),
        compiler_params=_cparams(("arbitrary",)),
        name="moe_gather",
    )(n_used, row_tok, row_tok, x1)


def _expert_changed(be_ref, i):
    return (i == 0) | (be_ref[i] != be_ref[jnp.maximum(i - 1, 0)])


def _stream_weight_tile(be_ref, nxt_ref, nused_ref, w_hbm, layer, col_starts, width, stage, dsts,
                        cnt_ref, sem):
    j = pl.program_id(0)
    i = pl.program_id(1)
    nu = nused_ref[0]

    def copies(e, jj, slot):
        return [pltpu.make_async_copy(
            w_hbm.at[layer, e, :, pl.ds(pl.multiple_of(c0 + jj * width, width), width)],
            stage.at[slot, n], sem.at[slot, n]) for n, c0 in enumerate(col_starts)]

    @pl.when((j == 0) & (i == 0))
    def _():
        cnt_ref[0] = 0
        for c in copies(be_ref[0], 0, 0):
            c.start()

    @pl.when((i < nu) & _expert_changed(be_ref, i))
    def _():
        k = cnt_ref[0]
        slot = k % 2
        for c in copies(be_ref[i], j, slot):
            c.wait()
        for n, dst in enumerate(dsts):
            dst[...] = stage[slot, n].astype(BF16)
        cnt_ref[0] = k + 1
        jn = jnp.where(be_ref[i] == be_ref[nu - 1], j + 1, j)

        @pl.when(jn < pl.num_programs(0))
        def _():
            for c in copies(nxt_ref[i], jn, 1 - slot):
                c.start()


def _expert_up_kernel(layer, be_ref, nxt_ref, nused_ref, x_ref, bg_ref, bu_ref, w_hbm, h_ref,
                      stage, wg_scr, wu_scr, cnt_ref, sem):
    i = pl.program_id(1)
    active = i < nused_ref[0]
    _stream_weight_tile(be_ref, nxt_ref, nused_ref, w_hbm, layer, (0, D_FF), MOE_TF, stage,
                        (wg_scr, wu_scr), cnt_ref, sem)

    @pl.when(active)
    def _():
        x = x_ref[...]
        gate = jnp.dot(x, wg_scr[...], preferred_element_type=F32) + bg_ref[0, 0]
        up = jnp.dot(x, wu_scr[...], preferred_element_type=F32) + bu_ref[0, 0]
        gate = jnp.minimum(gate, SWIGLU_LIMIT)
        up = jnp.clip(up, -SWIGLU_LIMIT, SWIGLU_LIMIT)
        h_ref[...] = ((up + 1.0) * (gate * jax.nn.sigmoid(SWIGLU_ALPHA * gate))).astype(h_ref.dtype)

    @pl.when(i >= nused_ref[0])
    def _():
        h_ref[...] = jnp.zeros_like(h_ref)


def _expert_up(xg, w_gu, b_gu, blk_e, nxt_e, n_used, layer):
    nf = D_FF // MOE_TF

    def used(i, nu):
        return jnp.minimum(i, nu[0] - 1)

    def bspec(col0):
        return pl.BlockSpec((1, 1, 1, MOE_TF),
                            lambda j, i, be, nx, nu: (layer, be[used(i, nu)], 0, col0 + j))

    return pl.pallas_call(
        functools.partial(_expert_up_kernel, layer),
        grid_spec=pltpu.PrefetchScalarGridSpec(
            num_scalar_prefetch=3,
            grid=(nf, MOE_NB),
            in_specs=[
                pl.BlockSpec((MOE_TB, D_MODEL), lambda j, i, be, nx, nu: (used(i, nu), 0)),
                bspec(0), bspec(nf), pl.BlockSpec(memory_space=pl.ANY),
            ],
            out_specs=pl.BlockSpec((MOE_TB, MOE_TF), lambda j, i, be, nx, nu: (i, j)),
            scratch_shapes=[pltpu.VMEM((2, 2, D_MODEL, MOE_TF), F32),
                            pltpu.VMEM((D_MODEL, MOE_TF), BF16), pltpu.VMEM((D_MODEL, MOE_TF), BF16),
                            pltpu.SMEM((1,), jnp.int32), pltpu.SemaphoreType.DMA((2, 2))],
        ),
        out_shape=jax.ShapeDtypeStruct((MOE_P, D_FF), BF16),
        compiler_params=_cparams(("arbitrary", "arbitrary"), VMEM_LIMIT),
        name="expert_up",
    )(blk_e, nxt_e, n_used, xg, b_gu, b_gu, w_gu)


def _expert_down_kernel(layer, be_ref, nxt_ref, nused_ref, h_ref, b_ref, w_hbm, y_ref,
                        stage, w_scr, cnt_ref, sem):
    i = pl.program_id(1)
    active = i < nused_ref[0]
    _stream_weight_tile(be_ref, nxt_ref, nused_ref, w_hbm, layer, (0,), MOE_TN, stage, (w_scr,),
                        cnt_ref, sem)

    @pl.when(active)
    def _():
        y_ref[...] = jnp.dot(h_ref[...], w_scr[...], preferred_element_type=F32) + b_ref[0, 0]

    @pl.when(i >= nused_ref[0])
    def _():
        y_ref[...] = jnp.zeros_like(y_ref)


def _expert_down(h, w_dn, b_dn, blk_e, nxt_e, n_used, layer):
    nn = D_MODEL // MOE_TN

    def used(i, nu):
        return jnp.minimum(i, nu[0] - 1)

    return pl.pallas_call(
        functools.partial(_expert_down_kernel, layer),
        grid_spec=pltpu.PrefetchScalarGridSpec(
            num_scalar_prefetch=3,
            grid=(nn, MOE_NB),
            in_specs=[
                pl.BlockSpec((MOE_TB, D_FF), lambda j, i, be, nx, nu: (used(i, nu), 0)),
                pl.BlockSpec((1, 1, 1, MOE_TN),
                             lambda j, i, be, nx, nu: (layer, be[used(i, nu)], 0, j)),
                pl.BlockSpec(memory_space=pl.ANY),
            ],
            out_specs=pl.BlockSpec((MOE_TB, MOE_TN), lambda j, i, be, nx, nu: (i, j)),
            scratch_shapes=[pltpu.VMEM((2, 1, D_FF, MOE_TN), F32), pltpu.VMEM((D_FF, MOE_TN), BF16),
                            pltpu.SMEM((1,), jnp.int32), pltpu.SemaphoreType.DMA((2, 1))],
        ),
        out_shape=jax.ShapeDtypeStruct((MOE_P, D_MODEL), F32),
        compiler_params=_cparams(("arbitrary", "arbitrary"), VMEM_LIMIT),
        name="expert_down",
    )(blk_e, nxt_e, n_used, h, b_dn, w_dn)


def _combine_kernel(pos_ref, pos_next_ref, gates_ref, x1_ref, y_hbm, g_ref, b_ref, x2_ref, x2b_ref,
                    buf, sem):
    i = pl.program_id(0)
    slot = i % 2
    n = CMB_T * TOP_K

    @pl.when(i == 0)
    def _():
        _issue_rows(y_hbm, pos_ref, n, buf.at[0], sem.at[0])

    @pl.when(i + 1 < pl.num_programs(0))
    def _():
        _issue_rows(y_hbm, pos_next_ref, n, buf.at[1 - slot], sem.at[1 - slot])

    _wait_rows(y_hbm, n, buf.at[slot], sem.at[slot])
    gates = gates_ref[...]
    y = jnp.zeros((CMB_T, D_MODEL), F32)
    for k in range(TOP_K):
        y = y + buf[slot, k * CMB_T:(k + 1) * CMB_T, :] * gates[:, k:k + 1]
    x2 = _layernorm(DN_ALPHA * x1_ref[...] + y, g_ref[...], b_ref[...])
    x2_ref[...] = x2
    x2b_ref[...] = x2.astype(BF16)


def _moe_combine(pos_kmajor, gates, x1, y, g, b):
    row = pl.BlockSpec((CMB_T, D_MODEL), lambda i: (i, 0))
    vec = pl.BlockSpec((1, D_MODEL), lambda i: (0, 0))
    steps = N_TOK // CMB_T
    return pl.pallas_call(
        _combine_kernel,
        grid=(steps,),
        in_specs=[pl.BlockSpec((CMB_T * TOP_K,), lambda i: (i,), memory_space=pltpu.SMEM),
                  pl.BlockSpec((CMB_T * TOP_K,), lambda i: (jnp.minimum(i + 1, steps - 1),),
                               memory_space=pltpu.SMEM),
                  pl.BlockSpec((CMB_T, TOP_K), lambda i: (i, 0)),
                  row, pl.BlockSpec(memory_space=pl.ANY), vec, vec],
        out_specs=[row, row],
        out_shape=[jax.ShapeDtypeStruct((N_TOK, D_MODEL), F32),
                   jax.ShapeDtypeStruct((N_TOK, D_MODEL), BF16)],
        scratch_shapes=[pltpu.VMEM((2, CMB_T * TOP_K, D_MODEL), F32),
                        pltpu.SemaphoreType.DMA((2,))],
        compiler_params=_cparams(("arbitrary",)),
        name="moe_combine",
    )(pos_kmajor, pos_kmajor, gates, x1, y, g, b)


def _permute_w_in(w):
    parts, off = [], 0
    for n in REF_IN_SIZES:
        parts.append((off, n))
        off += n
    return jnp.concatenate([w[:, parts[i][0]:parts[i][0] + parts[i][1]] for i in MY_ORDER],
                           axis=1).astype(BF16)


def kernel(x_prompt, x_sample, cache_conv, state_ret, cache_swa_k, cache_swa_v, cache_mem_k, cache_mem_v, mem_prompt, w_in, conv_w, ret_gn_w, attn_sinks, w_mem_kv, w_o, ln1_g, ln1_b, router_w, router_b, w_gate_up, b_gate_up, w_down, b_down, ln2_g, ln2_b):
    pos_p = jnp.arange(SEQ, dtype=jnp.int32)
    pos_s = PAST_LEN + jnp.arange(DEC_SEQ, dtype=jnp.int32)
    ret_tabs_p = _ret_tables(pos_p, R_CHUNK)
    ret_tabs_s = _ret_tables(pos_s, DEC_SEQ)
    swa_tabs_p = _swa_tables(pos_p)
    swa_tabs_s = _swa_tables(pos_s)
    kvw = A_KV * A_HD

    xp = x_prompt.reshape(N_PROMPT, D_MODEL)
    xs = x_sample.reshape(N_SAMPLE, D_MODEL)
    xb = jnp.concatenate([xp, xs], axis=0).astype(BF16)
    mem_b = mem_prompt.reshape(BATCH * N_MEM, D_MODEL).astype(BF16)
    zeros_e = jnp.zeros((256, D_MODEL), F32)
    b_gu = b_gate_up.reshape(DEPTH, N_EXP, 1, 2 * D_FF)
    b_dn = b_down.reshape(DEPTH, N_EXP, 1, D_MODEL)
    swa_k = cache_swa_k.reshape(DEPTH, DEC_BATCH, WINDOW, kvw)
    swa_v = cache_swa_v.reshape(DEPTH, DEC_BATCH, WINDOW, kvw)

    outs = {k: [] for k in ("p_conv", "p_ret", "p_k", "p_v", "p_mk", "p_mv",
                            "s_conv", "s_ret", "s_k", "s_v")}
    x2 = None
    for l in range(DEPTH):
        z = _matmul(xb, _permute_w_in(w_in[l]), 2304, 512)
        gn = ret_gn_w[l].reshape(1, D_MODEL)
        sinks = attn_sinks[l]

        ya_p, tail_p = _conv_branch(z, conv_w[l], zeros_e, zeros_e, 0, N_PROMPT, SEQ, False)
        yb_p, ret_p = _ret_prompt(z, ret_tabs_p, gn)
        yc_p, kr_p = _swa_prompt(z, sinks, swa_tabs_p)
        mkv = _matmul(mem_b, w_mem_kv[l].astype(BF16), 512, 512)
        mk_p = mkv[:, :D_MODEL].reshape(BATCH, N_MEM, D_MODEL)
        mv_p = mkv[:, D_MODEL:].reshape(BATCH, N_MEM, D_MODEL)
        ym_p = _mem_attend(z, mk_p, mv_p, 0, BATCH, SEQ, 512, "mem_prompt")

        ea = jnp.repeat(cache_conv[l][:, 0], DEC_SEQ, axis=0)
        eb = jnp.repeat(cache_conv[l][:, 1], DEC_SEQ, axis=0)
        ya_s, u_s = _conv_branch(z, conv_w[l], ea, eb, N_PROMPT, N_SAMPLE, DEC_SEQ, True)
        yb_s, ret_s = _ret_sample(z, ret_tabs_s, gn, state_ret, l)
        yc_s, nk_s, nv_s = _swa_sample(z, sinks, swa_tabs_s, swa_k, swa_v, l)
        ym_s = _mem_attend(z, cache_mem_k, cache_mem_v, N_PROMPT, DEC_BATCH, DEC_SEQ, DEC_SEQ,
                           "mem_sample", layer=l)

        wo_b = w_o[l].astype(BF16)
        g1 = ln1_g[l].reshape(1, D_MODEL)
        b1 = ln1_b[l].reshape(1, D_MODEL)
        rw = jnp.pad(router_w[l], ((0, 0), (0, LANES - N_EXP))).astype(BF16)
        rb = jnp.pad(router_b[l], (0, LANES - N_EXP)).reshape(1, LANES)
        x_p, x_s, xs_row0 = (xp, xs, 0) if l == 0 else (x2, x2, N_PROMPT)
        x1, lg = _wo_ln((ya_p, yb_p, yc_p, ym_p), x_p, (ya_s, yb_s, yc_s, ym_s), x_s, xs_row0,
                        wo_b, g1, b1, rw, rb)
        logits = lg[:, :N_EXP]

        gates, row_tok, pos, blk_e, nxt_e, n_used = _route(logits)
        xg = _moe_gather(x1, row_tok, n_used)
        h = _expert_up(xg, w_gate_up, b_gu, blk_e, nxt_e, n_used, l)
        y = _expert_down(h, w_down, b_dn, blk_e, nxt_e, n_used, l)
        pos_k = pos.reshape(N_TOK // CMB_T, CMB_T, TOP_K).transpose(0, 2, 1).reshape(-1)
        x2, xb = _moe_combine(pos_k, gates, x1, y, ln2_g[l].reshape(1, D_MODEL),
                              ln2_b[l].reshape(1, D_MODEL))

        tail_p = tail_p.reshape(BATCH, SEQ // 256, SUBLANES, D_MODEL)
        outs["p_conv"].append(tail_p[:, -1, -(CONV_W - 1):])
        outs["p_ret"].append(ret_p)
        outs["p_k"].append(kr_p.reshape(BATCH, SEQ, A_KV, A_HD)[:, -WINDOW:])
        outs["p_v"].append(z[:N_PROMPT, AV:AV + kvw].reshape(BATCH, SEQ, A_KV, A_HD)[:, -WINDOW:])
        outs["p_mk"].append(mk_p.reshape(BATCH, N_MEM, M_HEADS, M_HD))
        outs["p_mv"].append(mv_p.reshape(BATCH, N_MEM, M_HEADS, M_HD))
        outs["s_conv"].append(u_s.reshape(DEC_BATCH, DEC_SEQ, D_MODEL)[:, -(CONV_W - 1):])
        outs["s_ret"].append(ret_s)
        outs["s_k"].append(nk_s.reshape(DEC_BATCH, WINDOW, A_KV, A_HD))
        outs["s_v"].append(nv_s.reshape(DEC_BATCH, WINDOW, A_KV, A_HD))

    return (x2[:N_PROMPT].reshape(BATCH, SEQ, D_MODEL),
            x2[N_PROMPT:].reshape(DEC_BATCH, DEC_SEQ, D_MODEL),
            jnp.stack(outs["p_conv"]), jnp.stack(outs["p_ret"]), jnp.stack(outs["p_k"]),
            jnp.stack(outs["p_v"]), jnp.stack(outs["p_mk"]), jnp.stack(outs["p_mv"]),
            jnp.stack(outs["s_conv"]), jnp.stack(outs["s_ret"]), jnp.stack(outs["s_k"]),
            jnp.stack(outs["s_v"]))
```
